```python
import jax
import jax.numpy as jnp
from jax import lax
import numpy as np

D_MODEL = 2048
BATCH = 4
SEQ = 4096
DEPTH = 4

GRID_W = 64
CTX_LEN = 256
MIX_WIDTH = D_MODEL
CONV_CH = D_MODEL // 2
CONV_K = 31
FOURIER_CH = D_MODEL // 2
FOURIER_GROUPS = 4
MLA_NOPE = 128
MLA_ROPE = 64
MLA_V = 128
MLA_HEADS = (D_MODEL // 2) // MLA_V
Q_LORA = 512
KV_LORA = 512
SWA_HD = 64
SWA_HEADS = (D_MODEL // 2) // SWA_HD
SWA_KV_HEADS = 2
SWA_GROUP = SWA_HEADS // SWA_KV_HEADS
WINDOW = 128
Q_BLOCK = 128
FFN_HIDDEN = ((8 * D_MODEL + 3 * 256 - 1) // (3 * 256)) * 256
ROPE_BASE = 10000.0
NORM_EPS = 1e-6
LN_EPS = 1e-5
MLA_SCALE = (MLA_NOPE + MLA_ROPE) ** -0.5
SWA_SCALE = SWA_HD ** -0.5
EVEN_IN = 2 * CONV_CH + FOURIER_CH
ODD_IN = Q_LORA + KV_LORA + MLA_ROPE + (SWA_HEADS + 2 * SWA_KV_HEADS) * SWA_HD
N_EVEN = (DEPTH + 1) // 2
N_ODD = DEPTH // 2
ADA_INIT = 0.5

kernel_name = 'hybrid_conv_fourier_mla_swa_dit'


def rms_norm(x, g):
    xf = x.astype(jnp.float32)
    y = xf * lax.rsqrt(jnp.mean(jnp.square(xf), axis=-1, keepdims=True) + NORM_EPS)
    return (y * g.astype(jnp.float32)).astype(x.dtype)


def layer_norm(x, g, b):
    xf = x.astype(jnp.float32)
    mu = jnp.mean(xf, axis=-1, keepdims=True)
    var = jnp.mean(jnp.square(xf - mu), axis=-1, keepdims=True)
    y = (xf - mu) * lax.rsqrt(var + LN_EPS)
    return (y * g.astype(jnp.float32) + b.astype(jnp.float32)).astype(x.dtype)


def modulate(x, shift, scale):
    return x * (1 + scale) + shift


def axial_rope(x, row, col):
    r = x.shape[-1]
    half = r // 2
    quarter = r // 4
    inv = ROPE_BASE ** (-jnp.arange(quarter, dtype=jnp.float32) / quarter)
    bshape = (row.shape[0],) + (1,) * (x.ndim - 3) + (quarter,)

    def rot(v, pos):
        ang = (pos.astype(jnp.float32)[:, None] * inv).reshape(bshape)
        cs, sn = jnp.cos(ang), jnp.sin(ang)
        v1, v2 = v[..., :quarter], v[..., quarter:]
        return jnp.concatenate([v1 * cs - v2 * sn, v2 * cs + v1 * sn], axis=-1)

    xf = x.astype(jnp.float32)
    return jnp.concatenate([rot(xf[..., :half], row), rot(xf[..., half:], col)], axis=-1).astype(x.dtype)


def conv_fourier_mix(u, w_in, conv_w, conv_b, ln_g, ln_b):
    bsz, t, _ = u.shape
    h = u @ w_in
    a_val, a_gate, f = jnp.split(h, [CONV_CH, 2 * CONV_CH], axis=-1)
    a = a_val * jax.nn.sigmoid(a_gate)
    a = lax.conv_general_dilated(a, conv_w[:, None, :], window_strides=(1,),
                                 padding=[(CONV_K // 2, CONV_K // 2)],
                                 dimension_numbers=('NWC', 'WIO', 'NWC'),
                                 feature_group_count=CONV_CH) + conv_b
    a = jax.nn.silu(layer_norm(a, ln_g, ln_b))
    fg = f.astype(jnp.float32).reshape(bsz, t, FOURIER_GROUPS, FOURIER_CH // FOURIER_GROUPS)
    f = jnp.fft.fft2(fg, axes=(1, 3), norm='ortho').real.reshape(bsz, t, FOURIER_CH).astype(u.dtype)
    return jnp.concatenate([a, f], axis=-1)


def odd_project(u, w_in, q_norm_g, kv_norm_g, w_uq, w_uk, w_uv, row=None, col=None):
    bsz, t, _ = u.shape
    o1 = Q_LORA
    o2 = o1 + KV_LORA
    o3 = o2 + MLA_ROPE
    o4 = o3 + SWA_HEADS * SWA_HD
    o5 = o4 + SWA_KV_HEADS * SWA_HD
    cq, ckv, kr, qd, kd, vd = jnp.split(u @ w_in, [o1, o2, o3, o4, o5], axis=-1)
    q = (rms_norm(cq, q_norm_g) @ w_uq).reshape(bsz, t, MLA_HEADS, MLA_NOPE + MLA_ROPE)
    q_nope, q_rope = q[..., :MLA_NOPE], q[..., MLA_NOPE:]
    ckv = rms_norm(ckv, kv_norm_g)
    k_nope = (ckv @ w_uk).reshape(bsz, t, MLA_HEADS, MLA_NOPE)
    v_c = (ckv @ w_uv).reshape(bsz, t, MLA_HEADS, MLA_V)
    k_rope = kr
    q_s = qd.reshape(bsz, t, SWA_KV_HEADS, SWA_GROUP, SWA_HD)
    k_s = kd.reshape(bsz, t, SWA_KV_HEADS, SWA_HD)
    v_s = vd.reshape(bsz, t, SWA_KV_HEADS, SWA_HD)
    if row is not None:
        q_rope = axial_rope(q_rope, row, col)
        k_rope = axial_rope(k_rope[:, :, None, :], row, col)[:, :, 0, :]
        q_s = axial_rope(q_s, row, col)
        k_s = axial_rope(k_s, row, col)
    return q_nope, q_rope, k_nope, k_rope, v_c, q_s, k_s, v_s


def mla_attend(q_nope, q_rope, k_nope, k_rope, v):
    s = (jnp.einsum('bqhd,bkhd->bhqk', q_nope, k_nope)
         + jnp.einsum('bqhr,bkr->bhqk', q_rope, k_rope)).astype(jnp.float32) * MLA_SCALE
    p = jax.nn.softmax(s, axis=-1).astype(v.dtype)
    return jnp.einsum('bhqk,bkhd->bqhd', p, v)


def mla_latent(q_nope, q_rope, k_nope, k_rope, v):
    bsz, t = q_nope.shape[:2]
    nb = t // Q_BLOCK

    def blocks(a):
        return jnp.moveaxis(a.reshape((bsz, nb, Q_BLOCK) + a.shape[2:]), 1, 0)

    out = lax.map(lambda qs: mla_attend(qs[0], qs[1], k_nope, k_rope, v), (blocks(q_nope), blocks(q_rope)))
    return jnp.moveaxis(out, 0, 1).reshape(bsz, t, MLA_HEADS * MLA_V)


def swa_ctx(q, k, v, sink):
    bsz, t = q.shape[:2]
    s = jnp.einsum('bqkgd,bskd->bkgqs', q, k).astype(jnp.float32) * SWA_SCALE
    sink_col = jnp.broadcast_to(sink.astype(jnp.float32).reshape(1, SWA_KV_HEADS, SWA_GROUP, 1, 1), s.shape[:-1] + (1,))
    p = jax.nn.softmax(jnp.concatenate([s, sink_col], axis=-1), axis=-1)[..., :-1].astype(v.dtype)
    return jnp.einsum('bkgqs,bskd->bqkgd', p, v).reshape(bsz, t, SWA_HEADS * SWA_HD)


def swa_latent(q, k, v, ck, cv, sink):
    bsz, t = q.shape[:2]
    nb = t // Q_BLOCK
    span = Q_BLOCK + 2 * WINDOW
    pad = ((0, 0), (WINDOW, WINDOW), (0, 0), (0, 0))
    k_pad = jnp.pad(k, pad)
    v_pad = jnp.pad(v, pad)
    q_b = jnp.moveaxis(q.reshape(bsz, nb, Q_BLOCK, SWA_KV_HEADS, SWA_GROUP, SWA_HD), 1, 0)
    sink_f = sink.astype(jnp.float32).reshape(1, SWA_KV_HEADS, SWA_GROUP, 1, 1)

    def block(args):
        i, qb = args
        start = i * Q_BLOCK
        kb = lax.dynamic_slice_in_dim(k_pad, start, span, axis=1)
        vb = lax.dynamic_slice_in_dim(v_pad, start, span, axis=1)
        qpos = start + jnp.arange(Q_BLOCK)
        kpos = start - WINDOW + jnp.arange(span)
        valid = (kpos[None, :] >= 0) & (kpos[None, :] < t) & (jnp.abs(qpos[:, None] - kpos[None, :]) <= WINDOW)
        s_loc = jnp.einsum('bqkgd,bskd->bkgqs', qb, kb).astype(jnp.float32) * SWA_SCALE
        s_loc = jnp.where(valid, s_loc, -jnp.inf)
        s_ctx = jnp.einsum('bqkgd,bckd->bkgqc', qb, ck).astype(jnp.float32) * SWA_SCALE
        sink_col = jnp.broadcast_to(sink_f, s_ctx.shape[:-1] + (1,))
        p = jax.nn.softmax(jnp.concatenate([s_loc, s_ctx, sink_col], axis=-1), axis=-1).astype(v.dtype)
        return (jnp.einsum('bkgqs,bskd->bqkgd', p[..., :span], vb)
                + jnp.einsum('bkgqc,bckd->bqkgd', p[..., span:-1], cv))

    out = lax.map(block, (jnp.arange(nb), q_b))
    return jnp.moveaxis(out, 0, 1).reshape(bsz, t, SWA_HEADS * SWA_HD)


def attention_mix(u_lat, u_ctx, w_in, q_norm_g, kv_norm_g, w_uq, w_uk, w_uv, sink, row, col, need_ctx):
    qn, qr, kn, kr, vc, qs, ks, vs = odd_project(u_lat, w_in, q_norm_g, kv_norm_g, w_uq, w_uk, w_uv, row, col)
    cqn, cqr, ckn, ckr, cvc, cqs, cks, cvs = odd_project(u_ctx, w_in, q_norm_g, kv_norm_g, w_uq, w_uk, w_uv)
    y_mla = mla_latent(qn, qr, jnp.concatenate([kn, ckn], axis=1), jnp.concatenate([kr, ckr], axis=1),
                       jnp.concatenate([vc, cvc], axis=1))
    y_swa = swa_latent(qs, ks, vs, cks, cvs, sink)
    y_lat = jnp.concatenate([y_mla, y_swa], axis=-1)
    y_ctx = None
    if need_ctx:
        bsz, l = u_ctx.shape[:2]
        y_ctx = jnp.concatenate([mla_attend(cqn, cqr, ckn, ckr, cvc).reshape(bsz, l, MLA_HEADS * MLA_V),
                                 swa_ctx(cqs, cks, cvs, sink)], axis=-1)
    return y_lat, y_ctx


def swiglu(u, w_gate, w_up, w_down):
    return (jax.nn.silu(u @ w_gate) * (u @ w_up)) @ w_down


def setup_inputs(seed: int = 0) -> dict:
    key = jax.random.key(seed)
    ks = jax.random.split(key, 23)
    d = D_MODEL

    def nrm(k, shape, scale=1.0):
        return jax.random.normal(k, shape, jnp.float32) * scale

    return {
        'x': nrm(ks[0], (BATCH, SEQ, d)),
        'c': nrm(ks[1], (BATCH, d)),
        'ctx': nrm(ks[2], (BATCH, CTX_LEN, d)),
        'c_ctx': nrm(ks[3], (d,)),
        'w_ada': nrm(ks[4], (DEPTH, d, 6 * d), ADA_INIT * d ** -0.5),
        'b_ada': nrm(ks[5], (DEPTH, 6 * d), 0.02),
        'norm_g': 1.0 + nrm(ks[6], (DEPTH, 4, d), 0.02),
        'w_in_even': nrm(ks[7], (N_EVEN, d, EVEN_IN), d ** -0.5),
        'conv_w': nrm(ks[8], (N_EVEN, CONV_K, CONV_CH), CONV_K ** -0.5),
        'conv_b': nrm(ks[9], (N_EVEN, CONV_CH), 0.02),
        'conv_ln_g': 1.0 + nrm(ks[10], (N_EVEN, CONV_CH), 0.02),
        'conv_ln_b': nrm(ks[11], (N_EVEN, CONV_CH), 0.02),
        'w_in_odd': nrm(ks[12], (N_ODD, d, ODD_IN), d ** -0.5),
        'q_norm_g': 1.0 + nrm(ks[13], (N_ODD, Q_LORA), 0.02),
        'kv_norm_g': 1.0 + nrm(ks[14], (N_ODD, KV_LORA), 0.02),
        'w_uq': nrm(ks[15], (N_ODD, Q_LORA, MLA_HEADS * (MLA_NOPE + MLA_ROPE)), Q_LORA ** -0.5),
        'w_uk': nrm(ks[16], (N_ODD, KV_LORA, MLA_HEADS * MLA_NOPE), KV_LORA ** -0.5),
        'w_uv': nrm(ks[17], (N_ODD, KV_LORA, MLA_HEADS * MLA_V), KV_LORA ** -0.5),
        'sink': nrm(ks[18], (N_ODD, SWA_HEADS), 0.5),
        'w_out': nrm(ks[19], (DEPTH, MIX_WIDTH, d), MIX_WIDTH ** -0.5),
        'w_gate': nrm(ks[20], (DEPTH, d, FFN_HIDDEN), d ** -0.5),
        'w_up': nrm(ks[21], (DEPTH, d, FFN_HIDDEN), d ** -0.5),
        'w_down': nrm(ks[22], (DEPTH, FFN_HIDDEN, d), FFN_HIDDEN ** -0.5),
    }


def reference(x, c, ctx, c_ctx, w_ada, b_ada, norm_g, w_in_even, conv_w, conv_b, conv_ln_g, conv_ln_b,
              w_in_odd, q_norm_g, kv_norm_g, w_uq, w_uk, w_uv, sink, w_out, w_gate, w_up, w_down):
    t = x.shape[1]
    rows = t // GRID_W
    row = jnp.broadcast_to(jnp.arange(rows, dtype=jnp.int32)[:, None], (rows, GRID_W)).reshape(t)
    col = jnp.broadcast_to(jnp.arange(GRID_W, dtype=jnp.int32)[None, :], (rows, GRID_W)).reshape(t)
    c_act = jax.nn.silu(c)
    cc_act = jax.nn.silu(c_ctx)
    h_lat = x
    h_ctx = ctx
    for l in range(DEPTH):
        last = l == DEPTH - 1
        j = l // 2
        sh1, sc1, g1, sh2, sc2, g2 = jnp.split((c_act @ w_ada[l] + b_ada[l])[:, None, :], 6, axis=-1)
        csh1, csc1, cg1, csh2, csc2, cg2 = jnp.split((cc_act @ w_ada[l] + b_ada[l])[None, None, :], 6, axis=-1)
        u_lat = modulate(rms_norm(h_lat, norm_g[l, 0]), sh1, sc1)
        u_ctx = modulate(rms_norm(h_ctx, norm_g[l, 0]), csh1, csc1)
        if l % 2 == 0:
            y_lat = conv_fourier_mix(u_lat, w_in_even[j], conv_w[j], conv_b[j], conv_ln_g[j], conv_ln_b[j])
            y_ctx = None if last else conv_fourier_mix(u_ctx, w_in_even[j], conv_w[j], conv_b[j],
                                                       conv_ln_g[j], conv_ln_b[j])
        else:
            y_lat, y_ctx = attention_mix(u_lat, u_ctx, w_in_odd[j], q_norm_g[j], kv_norm_g[j], w_uq[j], w_uk[j],
                                         w_uv[j], sink[j], row, col, not last)
        h_lat = h_lat + g1 * rms_norm(y_lat @ w_out[l], norm_g[l, 1])
        u_lat = modulate(rms_norm(h_lat, norm_g[l, 2]), sh2, sc2)
        h_lat = h_lat + g2 * rms_norm(swiglu(u_lat, w_gate[l], w_up[l], w_down[l]), norm_g[l, 3])
        if not last:
            h_ctx = h_ctx + cg1 * rms_norm(y_ctx @ w_out[l], norm_g[l, 1])
            u_ctx = modulate(rms_norm(h_ctx, norm_g[l, 2]), csh2, csc2)
            h_ctx = h_ctx + cg2 * rms_norm(swiglu(u_ctx, w_gate[l], w_up[l], w_down[l]), norm_g[l, 3])
    return h_lat
```

```python
import functools

import jax
import jax.numpy as jnp
from jax import lax
from jax.experimental import pallas as pl
from jax.experimental.pallas import tpu as pltpu

F32 = jnp.float32
BF16 = jnp.bfloat16

D_MODEL = 2048
GRID_W = 64
CONV_CH = 1024
CONV_K = 31
FOURIER_CH = 1024
FOURIER_GROUPS = 4
FOURIER_GC = FOURIER_CH // FOURIER_GROUPS
MLA_NOPE = 128
MLA_ROPE = 64
MLA_V = 128
MLA_HEADS = 8
MLA_QK_PAD = 256
Q_LORA = 512
KV_LORA = 512
SWA_HD = 64
SWA_HEADS = 16
SWA_KV_HEADS = 2
SWA_GROUP = SWA_HEADS // SWA_KV_HEADS
WINDOW = 128
Q_BLOCK = 128
ROPE_BASE = 10000.0
NORM_EPS = 1e-6
LN_EPS = 1e-5
MLA_SCALE = (MLA_NOPE + MLA_ROPE) ** -0.5
SWA_SCALE = SWA_HD ** -0.5
LANE = 128
HALO = 16
MOD_ROWS = 8
VMEM_LIMIT = 56 * 1024 * 1024

TM = 512
TQ_MLA = 256
TT_CONV = 256
TK_DFT = 256
NEG_BIG = -1e30


def _cparams(*sem):
    return pltpu.CompilerParams(dimension_semantics=sem, vmem_limit_bytes=VMEM_LIMIT)


def _resident(shape):
    nd = len(shape)
    return pl.BlockSpec(shape, lambda *_: (0,) * nd, pipeline_mode=pl.Buffered(1))


def _rms(x, g):
    return x * lax.rsqrt(jnp.mean(x * x, axis=-1, keepdims=True) + NORM_EPS) * g


def _mod_spec(layer, chunk, mod_row_of_tile):
    def imap(i, *_):
        return ((layer * MOD_ROWS + mod_row_of_tile(i)) * 6 + chunk, 0, 0)
    return pl.BlockSpec((None, 1, D_MODEL), imap)


def _ada_kernel(c_ref, w_ref, b_ref, o_ref):
    c = c_ref[...]
    a = (c * jax.nn.sigmoid(c)).astype(BF16)
    o_ref[...] = jnp.dot(a, w_ref[...].astype(BF16), preferred_element_type=F32) + b_ref[...]


def _ada_mods(cvec, w_ada, b_ada):
    depth, d, n6 = w_ada.shape
    tn = 1024
    return pl.pallas_call(
        _ada_kernel,
        grid=(depth, n6 // tn),
        in_specs=[pl.BlockSpec((MOD_ROWS, d), lambda l, j: (0, 0)),
                  pl.BlockSpec((None, d, tn), lambda l, j: (l, 0, j)),
                  pl.BlockSpec((None, 1, tn), lambda l, j: (l, 0, j))],
        out_specs=pl.BlockSpec((None, MOD_ROWS, tn), lambda l, j: (l, 0, j)),
        out_shape=jax.ShapeDtypeStruct((depth, MOD_ROWS, n6), F32),
        compiler_params=_cparams("parallel", "parallel"),
        name="ada_mods",
    )(cvec, w_ada, b_ada.reshape(depth, 1, n6))


def _nm_matmul_kernel(h_ref, g_ref, sh_ref, sc_ref, w_ref, o_ref):
    u = _rms(h_ref[...], g_ref[...]) * (1.0 + sc_ref[...]) + sh_ref[...]
    o_ref[...] = jnp.dot(u.astype(BF16), w_ref[...], preferred_element_type=F32).astype(o_ref.dtype)


def _nm_matmul(h, g, mods, layer, w, n_rows, mod_row):
    nout = w.shape[1]
    return pl.pallas_call(
        _nm_matmul_kernel,
        grid=(n_rows // TM,),
        in_specs=[pl.BlockSpec((TM, D_MODEL), lambda i: (i, 0)),
                  _resident((1, D_MODEL)),
                  _mod_spec(layer, 0, mod_row), _mod_spec(layer, 1, mod_row),
                  _resident((D_MODEL, nout))],
        out_specs=pl.BlockSpec((TM, nout), lambda i: (i, 0)),
        out_shape=jax.ShapeDtypeStruct((n_rows, nout), BF16),
        compiler_params=_cparams("parallel"),
        name="nm_matmul",
    )(h, g, mods, mods, w)


def _out_proj_kernel(y1_ref, y2_ref, w1_ref, w2_ref, h_ref, g_ref, gate_ref, o_ref):
    y = (jnp.dot(y1_ref[...], w1_ref[...], preferred_element_type=F32)
         + jnp.dot(y2_ref[...], w2_ref[...], preferred_element_type=F32))
    o_ref[...] = h_ref[...] + gate_ref[...] * _rms(y, g_ref[...])


def _out_proj(y1, y2, w1, w2, h, g, mods, layer, n_rows, mod_row):
    half = y1.shape[1]
    return pl.pallas_call(
        _out_proj_kernel,
        grid=(n_rows // TM,),
        in_specs=[pl.BlockSpec((TM, half), lambda i: (i, 0)),
                  pl.BlockSpec((TM, half), lambda i: (i, 0)),
                  _resident((half, D_MODEL)), _resident((half, D_MODEL)),
                  pl.BlockSpec((TM, D_MODEL), lambda i: (i, 0)),
                  _resident((1, D_MODEL)),
                  _mod_spec(layer, 2, mod_row)],
        out_specs=pl.BlockSpec((TM, D_MODEL), lambda i: (i, 0)),
        out_shape=jax.ShapeDtypeStruct((n_rows, D_MODEL), F32),
        compiler_params=_cparams("parallel"),
        name="out_proj",
    )(y1, y2, w1, w2, h, g, mods)


def _ffn_kernel(h_ref, g2_ref, sh_ref, sc_ref, wg_ref, wu_ref, wd_ref, g3_ref, gate_ref, o_ref, u_ref, acc_ref):
    j = pl.program_id(1)

    @pl.when(j == 0)
    def _():
        u = _rms(h_ref[...], g2_ref[...]) * (1.0 + sc_ref[...]) + sh_ref[...]
        u_ref[...] = u.astype(BF16)
        acc_ref[...] = jnp.zeros_like(acc_ref)

    u = u_ref[...]
    a = jnp.dot(u, wg_ref[...], preferred_element_type=F32)
    b = jnp.dot(u, wu_ref[...], preferred_element_type=F32)
    hid = (a * jax.nn.sigmoid(a) * b).astype(BF16)
    acc_ref[...] += jnp.dot(hid, wd_ref[...], preferred_element_type=F32)

    @pl.when(j == pl.num_programs(1) - 1)
    def _():
        o_ref[...] = h_ref[...] + gate_ref[...] * _rms(acc_ref[...], g3_ref[...])


def _ffn(h, g2, g3, mods, layer, wg, wu, wd, n_rows, mod_row):
    hidden = wg.shape[1]
    th = 512
    return pl.pallas_call(
        _ffn_kernel,
        grid=(n_rows // TM, hidden // th),
        in_specs=[pl.BlockSpec((TM, D_MODEL), lambda i, j: (i, 0)),
                  _resident((1, D_MODEL)),
                  _mod_spec(layer, 3, mod_row), _mod_spec(layer, 4, mod_row),
                  pl.BlockSpec((D_MODEL, th), lambda i, j: (0, j)),
                  pl.BlockSpec((D_MODEL, th), lambda i, j: (0, j)),
                  pl.BlockSpec((th, D_MODEL), lambda i, j: (j, 0)),
                  _resident((1, D_MODEL)),
                  _mod_spec(layer, 5, mod_row)],
        out_specs=pl.BlockSpec((TM, D_MODEL), lambda i, j: (i, 0)),
        out_shape=jax.ShapeDtypeStruct((n_rows, D_MODEL), F32),
        scratch_shapes=[pltpu.VMEM((TM, D_MODEL), BF16), pltpu.VMEM((TM, D_MODEL), F32)],
        compiler_params=_cparams("parallel", "arbitrary"),
        name="ffn",
    )(h, g2, mods, mods, wg, wu, wd, g3, mods)


def _conv_kernel(v_ref, gt_ref, vp_ref, gp_ref, vn_ref, gn_ref, w_ref, cb_ref, lg_ref, lb_ref, o_ref, buf_ref,
                 *, lat_tiles, tiles_per_seq):
    i = pl.program_id(0)
    tt = v_ref.shape[0]
    pos = i % tiles_per_seq
    is_lat = i < lat_tiles
    has_prev = jnp.logical_and(is_lat, pos != 0)
    has_next = jnp.logical_and(is_lat, pos != tiles_per_seq - 1)

    def glu(v, g):
        return v[...].astype(F32) * jax.nn.sigmoid(g[...].astype(F32))

    buf_ref[0:HALO, :] = jnp.where(has_prev, glu(vp_ref, gp_ref), 0.0)
    buf_ref[HALO:HALO + tt, :] = glu(v_ref, gt_ref)
    buf_ref[HALO + tt:, :] = jnp.where(has_next, glu(vn_ref, gn_ref), 0.0)

    rc = 32
    off = HALO - CONV_K // 2
    for r in range(tt // rc):
        acc = jnp.broadcast_to(cb_ref[...], (rc, CONV_CH))
        for k in range(CONV_K):
            acc = acc + w_ref[k:k + 1, :] * buf_ref[r * rc + k + off:r * rc + k + off + rc, :]
        mu = jnp.mean(acc, axis=-1, keepdims=True)
        xc = acc - mu
        var = jnp.mean(xc * xc, axis=-1, keepdims=True)
        y = xc * lax.rsqrt(var + LN_EPS) * lg_ref[...] + lb_ref[...]
        o_ref[r * rc:(r + 1) * rc, :] = (y * jax.nn.sigmoid(y)).astype(o_ref.dtype)


def _conv_branch(p, conv_w, conv_b, ln_g, ln_b, n_lat, seq_lat, n_rows):
    tt = TT_CONV
    hb = tt // HALO
    last_hb = n_rows // HALO - 1
    kern = functools.partial(_conv_kernel, lat_tiles=n_lat // tt, tiles_per_seq=seq_lat // tt)

    def prev_map(col):
        return lambda i: (jnp.maximum(i * hb - 1, 0), col)

    def next_map(col):
        return lambda i: (jnp.minimum((i + 1) * hb, last_hb), col)

    return pl.pallas_call(
        kern,
        grid=(n_rows // tt,),
        in_specs=[pl.BlockSpec((tt, CONV_CH), lambda i: (i, 0)),
                  pl.BlockSpec((tt, CONV_CH), lambda i: (i, 1)),
                  pl.BlockSpec((HALO, CONV_CH), prev_map(0)),
                  pl.BlockSpec((HALO, CONV_CH), prev_map(1)),
                  pl.BlockSpec((HALO, CONV_CH), next_map(0)),
                  pl.BlockSpec((HALO, CONV_CH), next_map(1)),
                  _resident((CONV_K, CONV_CH)),
                  _resident((1, CONV_CH)), _resident((1, CONV_CH)), _resident((1, CONV_CH))],
        out_specs=pl.BlockSpec((tt, CONV_CH), lambda i: (i, 0)),
        out_shape=jax.ShapeDtypeStruct((n_rows, CONV_CH), BF16),
        scratch_shapes=[pltpu.VMEM((tt + 2 * HALO, CONV_CH), F32)],
        compiler_params=_cparams("parallel"),
        name="conv_branch",
    )(p, p, p, p, p, p, conv_w, conv_b, ln_g, ln_b)


def _dft_ch_kernel(f_ref, cs_ref, zc_ref, zs_ref):
    z = jnp.dot(f_ref[...], cs_ref[...], preferred_element_type=F32)
    zc_ref[...] = z[:, :FOURIER_GC].astype(zc_ref.dtype)
    zs_ref[...] = z[:, FOURIER_GC:].astype(zs_ref.dtype)


def _dft_channels(p, cs_c, n_rows):
    col0 = (2 * CONV_CH) // FOURIER_GC
    out = jax.ShapeDtypeStruct((n_rows, FOURIER_CH), BF16)
    return pl.pallas_call(
        _dft_ch_kernel,
        grid=(n_rows // TM, FOURIER_GROUPS),
        in_specs=[pl.BlockSpec((TM, FOURIER_GC), lambda i, g: (i, col0 + g)),
                  _resident((FOURIER_GC, 2 * FOURIER_GC))],
        out_specs=[pl.BlockSpec((TM, FOURIER_GC), lambda i, g: (i, g)),
                   pl.BlockSpec((TM, FOURIER_GC), lambda i, g: (i, g))],
        out_shape=[out, out],
        compiler_params=_cparams("parallel", "parallel"),
        name="dft_channels",
    )(p, cs_c)


def _dft_pos_kernel(ct_ref, st_ref, zc_ref, zs_ref, o_ref):
    y = (jnp.dot(ct_ref[...], zc_ref[...], preferred_element_type=F32)
         + jnp.dot(st_ref[...], zs_ref[...], preferred_element_type=F32))
    o_ref[...] = y.astype(o_ref.dtype)


def _dft_positions(ct, nst, zc, zs, row0, seq, nbatch, out_rows):
    tk = min(TK_DFT, seq)
    sb0 = row0 // seq
    ob0 = row0 // tk
    return pl.pallas_call(
        _dft_pos_kernel,
        grid=(nbatch, seq // tk),
        in_specs=[pl.BlockSpec((tk, seq), lambda b, k: (k, 0)),
                  pl.BlockSpec((tk, seq), lambda b, k: (k, 0)),
                  pl.BlockSpec((seq, FOURIER_CH), lambda b, k: (sb0 + b, 0)),
                  pl.BlockSpec((seq, FOURIER_CH), lambda b, k: (sb0 + b, 0))],
        out_specs=pl.BlockSpec((tk, FOURIER_CH), lambda b, k: (ob0 + b * (seq // tk) + k, 0)),
        out_shape=jax.ShapeDtypeStruct((out_rows, FOURIER_CH), BF16),
        compiler_params=_cparams("parallel", "arbitrary"),
        name="dft_positions",
    )(ct, nst, zc, zs)


def _dft_tables(n, scale):
    k = jnp.arange(n, dtype=jnp.int32)
    m = (k[:, None] * k[None, :]) % n
    ang = m.astype(F32) * (2.0 * jnp.pi / n)
    return (jnp.cos(ang) * scale).astype(BF16), (-jnp.sin(ang) * scale).astype(BF16)


def _dft_tables_big(n, scale):
    r = int(round(n ** 0.5))
    assert r * r == n
    t = jnp.arange(n, dtype=jnp.int32)[None, :]
    kk = jnp.arange(r, dtype=jnp.int32)[:, None]
    a_hi = ((kk * r * t) % n).astype(F32) * (2.0 * jnp.pi / n)
    a_lo = ((kk * t) % n).astype(F32) * (2.0 * jnp.pi / n)
    ch, sh, cl, sl = jnp.cos(a_hi)[:, None, :], jnp.sin(a_hi)[:, None, :], jnp.cos(a_lo)[None], jnp.sin(a_lo)[None]
    c = (ch * cl - sh * sl) * scale
    s = (sh * cl + ch * sl) * scale
    return c.reshape(n, n).astype(BF16), (-s).reshape(n, n).astype(BF16)


def _rope(x, cos, sin_signed, first_half):
    swapped = jnp.where(first_half, pltpu.roll(x, LANE - 16, 1), pltpu.roll(x, 16, 1))
    return x * cos + swapped * sin_signed


def _odd_proj_kernel(p_ref, cos_ref, sin_ref, gq_ref, gkv_ref, wuq_ref, wuk_ref, wuv_ref,
                     q_ref, k_ref, v_ref, qs_ref, ks_ref, vs_ref):
    tm = p_ref.shape[0]
    cos = cos_ref[...]
    sin = sin_ref[...]
    lane = lax.broadcasted_iota(jnp.int32, (tm, LANE), 1)
    first_half = (lane % 32) < 16
    rope = functools.partial(_rope, cos=cos, sin_signed=sin, first_half=first_half)

    o_kv = Q_LORA
    o_kr = o_kv + KV_LORA
    o_qs = o_kr + LANE
    o_ks = o_qs + SWA_HEADS * SWA_HD
    o_vs = o_ks + SWA_KV_HEADS * SWA_HD

    nq = _rms(p_ref[:, 0:Q_LORA].astype(F32), gq_ref[...]).astype(BF16)
    nkv = _rms(p_ref[:, o_kv:o_kr].astype(F32), gkv_ref[...]).astype(BF16)
    q = jnp.dot(nq, wuq_ref[...], preferred_element_type=F32) * MLA_SCALE
    kn = jnp.dot(nkv, wuk_ref[...], preferred_element_type=F32)
    v_ref[...] = jnp.dot(nkv, wuv_ref[...], preferred_element_type=F32).astype(v_ref.dtype)
    kr = rope(p_ref[:, o_kr:o_qs].astype(F32)).astype(k_ref.dtype)
    for h in range(MLA_HEADS):
        c0 = h * MLA_QK_PAD
        q_ref[:, c0:c0 + LANE] = q[:, c0:c0 + LANE].astype(q_ref.dtype)
        q_ref[:, c0 + LANE:c0 + 2 * LANE] = rope(q[:, c0 + LANE:c0 + 2 * LANE]).astype(q_ref.dtype)
        k_ref[:, c0:c0 + LANE] = kn[:, h * MLA_NOPE:(h + 1) * MLA_NOPE].astype(k_ref.dtype)
        k_ref[:, c0 + LANE:c0 + 2 * LANE] = kr
    for j in range(SWA_HEADS * SWA_HD // LANE):
        x = p_ref[:, o_qs + j * LANE:o_qs + (j + 1) * LANE].astype(F32) * SWA_SCALE
        qs_ref[:, j * LANE:(j + 1) * LANE] = rope(x).astype(qs_ref.dtype)
    ks_ref[...] = rope(p_ref[:, o_ks:o_vs].astype(F32)).astype(ks_ref.dtype)
    vs_ref[...] = p_ref[:, o_vs:o_vs + LANE]


def _odd_proj(p, cos_t, sin_t, gq, gkv, wuq, wuk, wuv, n_lat, seq_lat, n_rows):
    tm = TM
    lat_tiles = n_lat // tm
    per_seq = seq_lat // tm

    def tab_map(i):
        return (jnp.where(i < lat_tiles, i % per_seq, per_seq), 0)

    def rows(w):
        return pl.BlockSpec((tm, w), lambda i: (i, 0))

    def out(w):
        return jax.ShapeDtypeStruct((n_rows, w), BF16)

    qk_w = MLA_HEADS * MLA_QK_PAD
    return pl.pallas_call(
        _odd_proj_kernel,
        grid=(n_rows // tm,),
        in_specs=[rows(p.shape[1]),
                  pl.BlockSpec((tm, LANE), tab_map), pl.BlockSpec((tm, LANE), tab_map),
                  _resident((1, Q_LORA)), _resident((1, KV_LORA)),
                  _resident(wuq.shape), _resident(wuk.shape), _resident(wuv.shape)],
        out_specs=[rows(qk_w), rows(qk_w), rows(MLA_HEADS * MLA_V), rows(SWA_HEADS * SWA_HD), rows(LANE), rows(LANE)],
        out_shape=[out(qk_w), out(qk_w), out(MLA_HEADS * MLA_V), out(SWA_HEADS * SWA_HD), out(LANE), out(LANE)],
        compiler_params=_cparams("parallel"),
        name="odd_proj",
    )(p, cos_t, sin_t, gq, gkv, wuq, wuk, wuv)


def _rope_tables(seq, pad_rows):
    quarter = MLA_ROPE // 4
    inv = ROPE_BASE ** (-jnp.arange(quarter, dtype=F32) / quarter)
    t = jnp.arange(seq, dtype=jnp.int32)
    row = (t // GRID_W).astype(F32)[:, None] * inv
    col = (t % GRID_W).astype(F32)[:, None] * inv
    ang = jnp.concatenate([row, row, col, col], axis=-1)
    sign = jnp.concatenate([-jnp.ones((quarter,), F32), jnp.ones((quarter,), F32)] * 2)
    cos = jnp.cos(ang)
    sin = jnp.sin(ang) * sign
    cos = jnp.concatenate([jnp.tile(cos, (1, 2)), jnp.ones((pad_rows, LANE), F32)], axis=0)
    sin = jnp.concatenate([jnp.tile(sin, (1, 2)), jnp.zeros((pad_rows, LANE), F32)], axis=0)
    return cos, sin


def _softmax_pv(scores, values, extra_logit=None):
    m = functools.reduce(jnp.maximum, [jnp.max(s, axis=-1, keepdims=True) for s in scores])
    if extra_logit is not None:
        m = jnp.maximum(m, extra_logit)
    ps = [jnp.exp(s - m) for s in scores]
    l = functools.reduce(jnp.add, [jnp.sum(p, axis=-1, keepdims=True) for p in ps])
    if extra_logit is not None:
        l = l + jnp.exp(extra_logit - m)
    o = functools.reduce(jnp.add, [jnp.dot(p.astype(BF16), v, preferred_element_type=F32)
                                   for p, v in zip(ps, values)])
    return o / l


def _qk(q, k):
    return lax.dot_general(q, k, (((1,), (1,)), ((), ())), preferred_element_type=F32)


def _mla_kernel(q_ref, kl_ref, kc_ref, vl_ref, vc_ref, o_ref, *, lat_q_tiles, ctx_queries):
    q = q_ref[...]

    def lat_query():
        o = _softmax_pv([_qk(q, kl_ref[...]), _qk(q, kc_ref[...])], [vl_ref[...], vc_ref[...]])
        o_ref[...] = o.astype(o_ref.dtype)

    def ctx_query():
        o = _softmax_pv([_qk(q, kc_ref[...])], [vc_ref[...]])
        o_ref[...] = o.astype(o_ref.dtype)

    if ctx_queries:
        is_ctx = pl.program_id(2) >= lat_q_tiles
        pl.when(is_ctx)(ctx_query)
        pl.when(jnp.logical_not(is_ctx))(lat_query)
    else:
        lat_query()


def _mla_attention(q, k, v, nbatch, seq_lat, seq_ctx, ctx_queries):
    tq = TQ_MLA
    n_lat = nbatch * seq_lat
    lat_q_tiles = seq_lat // tq
    ctx_q_tiles = seq_ctx // tq if ctx_queries else 0
    out_rows = n_lat + (nbatch * seq_ctx if ctx_queries else 0)
    ctx_blk0 = n_lat // seq_ctx

    def q_map(b, h, i):
        return (jnp.where(i < lat_q_tiles, b * lat_q_tiles + i,
                          n_lat // tq + b * ctx_q_tiles + (i - lat_q_tiles)), h)

    kern = functools.partial(_mla_kernel, lat_q_tiles=lat_q_tiles, ctx_queries=ctx_queries)
    return pl.pallas_call(
        kern,
        grid=(nbatch, MLA_HEADS, lat_q_tiles + ctx_q_tiles),
        in_specs=[pl.BlockSpec((tq, MLA_QK_PAD), q_map),
                  pl.BlockSpec((seq_lat, MLA_QK_PAD), lambda b, h, i: (b, h)),
                  pl.BlockSpec((seq_ctx, MLA_QK_PAD), lambda b, h, i: (ctx_blk0 + b, h)),
                  pl.BlockSpec((seq_lat, MLA_V), lambda b, h, i: (b, h)),
                  pl.BlockSpec((seq_ctx, MLA_V), lambda b, h, i: (ctx_blk0 + b, h))],
        out_specs=pl.BlockSpec((tq, MLA_V), q_map),
        out_shape=jax.ShapeDtypeStruct((out_rows, MLA_HEADS * MLA_V), BF16),
        compiler_params=_cparams("parallel", "parallel", "arbitrary"),
        name="mla_attention",
    )(q, k, k, v, v)


def _swa_kernel(sink_ref, q_ref, kp_ref, kc_ref, kn_ref, kx_ref, vp_ref, vc_ref, vn_ref, vx_ref, o_ref,
                *, lat_q_blocks, seq_lat, ctx_queries):
    i = pl.program_id(1)
    q = q_ref[...]
    span = 3 * Q_BLOCK

    def heads(k_all, v_all, valid):
        outs = []
        for h in range(SWA_HEADS):
            kv = h // SWA_GROUP
            kh = k_all[:, kv * SWA_HD:(kv + 1) * SWA_HD]
            vh = v_all[:, kv * SWA_HD:(kv + 1) * SWA_HD]
            s = _qk(q[:, h * SWA_HD:(h + 1) * SWA_HD], kh)
            if valid is not None:
                s = jnp.where(valid, s, NEG_BIG)
            outs.append(_softmax_pv([s], [vh], extra_logit=sink_ref[h]))
        o_ref[...] = jnp.concatenate(outs, axis=-1).astype(o_ref.dtype)

    def lat_query():
        k_all = jnp.concatenate([kp_ref[...], kc_ref[...], kn_ref[...], kx_ref[...]], axis=0)
        v_all = jnp.concatenate([vp_ref[...], vc_ref[...], vn_ref[...], vx_ref[...]], axis=0)
        nk = k_all.shape[0]
        qpos = i * Q_BLOCK + lax.broadcasted_iota(jnp.int32, (Q_BLOCK, nk), 0)
        col = lax.broadcasted_iota(jnp.int32, (Q_BLOCK, nk), 1)
        kpos = (i - 1) * Q_BLOCK + col
        in_window = (kpos >= 0) & (kpos < seq_lat) & (jnp.abs(qpos - kpos) <= WINDOW)
        heads(k_all, v_all, (col >= span) | in_window)

    def ctx_query():
        heads(kx_ref[...], vx_ref[...], None)

    if ctx_queries:
        is_ctx = i >= lat_q_blocks
        pl.when(is_ctx)(ctx_query)
        pl.when(jnp.logical_not(is_ctx))(lat_query)
    else:
        lat_query()


def _swa_attention(sink, qs, ks, vs, nbatch, seq_lat, seq_ctx, ctx_queries):
    qb = Q_BLOCK
    n_lat = nbatch * seq_lat
    lat_q_blocks = seq_lat // qb
    ctx_q_blocks = seq_ctx // qb if ctx_queries else 0
    out_rows = n_lat + (nbatch * seq_ctx if ctx_queries else 0)
    ctx_blk0 = n_lat // seq_ctx

    def q_map(b, i):
        return (jnp.where(i < lat_q_blocks, b * lat_q_blocks + i,
                          n_lat // qb + b * ctx_q_blocks + (i - lat_q_blocks)), 0)

    def k_map(delta):
        def imap(b, i):
            return (b * lat_q_blocks + jnp.clip(i + delta, 0, lat_q_blocks - 1), 0)
        return imap

    kblk = [pl.BlockSpec((qb, LANE), k_map(d)) for d in (-1, 0, 1)]
    xblk = pl.BlockSpec((seq_ctx, LANE), lambda b, i: (ctx_blk0 + b, 0))
    kern = functools.partial(_swa_kernel, lat_q_blocks=lat_q_blocks, seq_lat=seq_lat, ctx_queries=ctx_queries)
    return pl.pallas_call(
        kern,
        grid=(nbatch, lat_q_blocks + ctx_q_blocks),
        in_specs=[pl.BlockSpec(memory_space=pltpu.SMEM),
                  pl.BlockSpec((qb, SWA_HEADS * SWA_HD), q_map)] + kblk + [xblk] + kblk + [xblk],
        out_specs=pl.BlockSpec((qb, SWA_HEADS * SWA_HD), q_map),
        out_shape=jax.ShapeDtypeStruct((out_rows, SWA_HEADS * SWA_HD), BF16),
        compiler_params=_cparams("parallel", "arbitrary"),
        name="swa_attention",
    )(sink, qs, ks, ks, ks, ks, vs, vs, vs, vs)


def _pad_cols(w, width):
    return jnp.pad(w, ((0, 0), (0, width - w.shape[1])))


def _odd_in_layout(w):
    o2 = Q_LORA + KV_LORA
    o3 = o2 + MLA_ROPE
    return jnp.concatenate([w[:, :o2], _pad_cols(w[:, o2:o3], LANE), w[:, o3:]], axis=1).astype(BF16)


def _uq_layout(w):
    per = MLA_NOPE + MLA_ROPE
    w = w.reshape(w.shape[0], MLA_HEADS, per)
    w = jnp.pad(w, ((0, 0), (0, 0), (0, MLA_QK_PAD - per)))
    return w.reshape(w.shape[0], MLA_HEADS * MLA_QK_PAD).astype(BF16)


def kernel(x, c, ctx, c_ctx, w_ada, b_ada, norm_g, w_in_even, conv_w, conv_b, conv_ln_g, conv_ln_b,
           w_in_odd, q_norm_g, kv_norm_g, w_uq, w_uk, w_uv, sink, w_out, w_gate, w_up, w_down):
    nbatch, seq, d = x.shape
    seq_ctx = ctx.shape[1]
    depth = w_ada.shape[0]
    n_lat = nbatch * seq
    n_all = n_lat + nbatch * seq_ctx
    lat_tiles = n_lat // TM
    tiles_per_batch = seq // TM

    def mod_row(i):
        return jnp.where(i < lat_tiles, i // tiles_per_batch, nbatch)

    cvec = jnp.concatenate([c, c_ctx[None, :], jnp.zeros((MOD_ROWS - nbatch - 1, d), F32)], axis=0)
    mods = _ada_mods(cvec, w_ada, b_ada).reshape(depth * MOD_ROWS * 6, 1, d)

    h = jnp.concatenate([x.reshape(n_lat, d), ctx.reshape(nbatch * seq_ctx, d)], axis=0)

    ortho = 1.0 / float((seq * FOURIER_GC) ** 0.5)
    ortho_ctx = 1.0 / float((seq_ctx * FOURIER_GC) ** 0.5)
    ct_lat, nst_lat = _dft_tables_big(seq, ortho)
    ct_ctx, nst_ctx = _dft_tables(seq_ctx, ortho_ctx)
    cc, ncs = _dft_tables(FOURIER_GC, 1.0)
    cs_c = jnp.concatenate([cc, -ncs], axis=1)
    cos_t, sin_t = _rope_tables(seq, TM)

    for l in range(depth):
        last = l == depth - 1
        j = l // 2
        g = norm_g[l][:, None, :]
        rows_out = n_lat if last else n_all
        if l % 2 == 0:
            p = _nm_matmul(h, g[0], mods, l, w_in_even[j].astype(BF16), n_all, mod_row)
            y1 = _conv_branch(p, conv_w[j], conv_b[j][None], conv_ln_g[j][None], conv_ln_b[j][None],
                              n_lat, seq, n_all)
            zc, zs = _dft_channels(p, cs_c, n_all)
            y2_lat = _dft_positions(ct_lat, nst_lat, zc, zs, 0, seq, nbatch, n_lat)
            y2_ctx = _dft_positions(ct_ctx, nst_ctx, zc, zs, n_lat, seq_ctx, nbatch, n_all)
            y2 = jnp.concatenate([y2_lat, y2_ctx[n_lat:]], axis=0)
        else:
            p = _nm_matmul(h, g[0], mods, l, _odd_in_layout(w_in_odd[j]), n_all, mod_row)
            q, k, v, qs, ks, vs = _odd_proj(p, cos_t, sin_t, q_norm_g[j][None], kv_norm_g[j][None],
                                            _uq_layout(w_uq[j]), w_uk[j].astype(BF16), w_uv[j].astype(BF16),
                                            n_lat, seq, n_all)
            y1 = _mla_attention(q, k, v, nbatch, seq, seq_ctx, not last)
            y2 = _swa_attention(sink[j], qs, ks, vs, nbatch, seq, seq_ctx, not last)
        wo = w_out[l].astype(BF16)
        half = wo.shape[0] // 2
        h = _out_proj(y1, y2, wo[:half], wo[half:], h, g[1], mods, l, rows_out, mod_row)
        h = _ffn(h, g[2], g[3], mods, l, w_gate[l].astype(BF16), w_up[l].astype(BF16), w_down[l].astype(BF16),
                 rows_out, mod_row)
    return h.reshape(nbatch, seq, d)
```

```python
import functools
import math

import jax
import jax.numpy as jnp
from jax import lax
from jax.experimental import pallas as pl
from jax.experimental.pallas import tpu as pltpu

F32 = jnp.float32
BF16 = jnp.bfloat16

D_MODEL = 2048
GRID_W = 64
CONV_CH = 1024
CONV_K = 31
FOURIER_CH = 1024
FOURIER_GROUPS = 4
FOURIER_GC = FOURIER_CH // FOURIER_GROUPS
MLA_NOPE = 128
MLA_ROPE = 64
MLA_V = 128
MLA_HEADS = 8
MLA_PAD = 256
Q_LORA = 512
KV_LORA = 512
SWA_HD = 64
SWA_HEADS = 16
SWA_KV_HEADS = 2
SWA_GROUP = SWA_HEADS // SWA_KV_HEADS
WINDOW = 128
Q_BLOCK = 128
ROPE_BASE = 10000.0
NORM_EPS = 1e-6
LN_EPS = 1e-5
LOG2E = math.log2(math.e)
MLA_QSCALE = (MLA_NOPE + MLA_ROPE) ** -0.5 * LOG2E
SWA_QSCALE = SWA_HD ** -0.5 * LOG2E
LANE = 128
SUBLANE = 8
HALO = 16
MOD_ROWS = 8
VMEM_LIMIT = 56 * 1024 * 1024

TM = 512
TH_FFN = 512
TQ_MLA = 256
MLA_HEADS_PER_STEP = 2
TT_CONV = 256
TK_DFT = 256
NEG_BIG = -1e30


def _cparams(*sem):
    return pltpu.CompilerParams(dimension_semantics=sem, vmem_limit_bytes=VMEM_LIMIT)


def _resident(shape):
    nd = len(shape)
    return pl.BlockSpec(shape, lambda *_: (0,) * nd, pipeline_mode=pl.Buffered(1))


def _layer_resident(shape, layer, block=0):
    return pl.BlockSpec((None,) + tuple(shape), lambda *_: (layer, block, 0), pipeline_mode=pl.Buffered(1))


def _rms(x, g):
    return x * lax.rsqrt(jnp.mean(x * x, axis=-1, keepdims=True) + NORM_EPS) * g


def _mod_spec(layer, chunk, mod_row_of_tile):
    def imap(i, *_):
        return ((layer * MOD_ROWS + mod_row_of_tile(i)) * 6 + chunk, 0, 0)
    return pl.BlockSpec((None, 1, D_MODEL), imap)


def _ada_kernel(c_ref, w_ref, b_ref, o_ref):
    c = c_ref[...]
    a = (c * jax.nn.sigmoid(c)).astype(BF16)
    o_ref[...] = jnp.dot(a, w_ref[...].astype(BF16), preferred_element_type=F32) + b_ref[...]


def _ada_mods(cvec, w_ada, b_ada):
    depth, d, n6 = w_ada.shape
    tn = 1024
    return pl.pallas_call(
        _ada_kernel,
        grid=(depth, n6 // tn),
        in_specs=[pl.BlockSpec((MOD_ROWS, d), lambda l, j: (0, 0)),
                  pl.BlockSpec((None, d, tn), lambda l, j: (l, 0, j)),
                  pl.BlockSpec((None, 1, tn), lambda l, j: (l, 0, j))],
        out_specs=pl.BlockSpec((None, MOD_ROWS, tn), lambda l, j: (l, 0, j)),
        out_shape=jax.ShapeDtypeStruct((depth, MOD_ROWS, n6), F32),
        compiler_params=_cparams("parallel", "parallel"),
        name="ada_mods",
    )(cvec, w_ada, b_ada.reshape(depth, 1, n6))


def _nm_matmul_kernel(h_ref, g_ref, sh_ref, sc_ref, w_ref, o_ref):
    u = _rms(h_ref[...], g_ref[...]) * (1.0 + sc_ref[...]) + sh_ref[...]
    o_ref[...] = jnp.dot(u.astype(BF16), w_ref[...], preferred_element_type=F32).astype(o_ref.dtype)


def _nm_matmul(h, g, mods, layer, w_stack, w_layer, n_rows, mod_row):
    nout = w_stack.shape[2]
    return pl.pallas_call(
        _nm_matmul_kernel,
        grid=(n_rows // TM,),
        in_specs=[pl.BlockSpec((TM, D_MODEL), lambda i: (i, 0)),
                  _resident((1, D_MODEL)),
                  _mod_spec(layer, 0, mod_row), _mod_spec(layer, 1, mod_row),
                  _layer_resident((D_MODEL, nout), w_layer)],
        out_specs=pl.BlockSpec((TM, nout), lambda i: (i, 0)),
        out_shape=jax.ShapeDtypeStruct((n_rows, nout), BF16),
        compiler_params=_cparams("parallel"),
        name="nm_matmul",
    )(h, g, mods, mods, w_stack)


def _out_proj_kernel(y1_ref, y2_ref, w1_ref, w2_ref, h_ref, g_ref, gate_ref, o_ref):
    y = (jnp.dot(y1_ref[...], w1_ref[...], preferred_element_type=F32)
         + jnp.dot(y2_ref[...], w2_ref[...], preferred_element_type=F32))
    o_ref[...] = h_ref[...] + gate_ref[...] * _rms(y, g_ref[...])


def _out_proj(y1, y2, w_stack, w_layer, h, g, mods, layer, n_rows, mod_row):
    half = y1.shape[1]
    return pl.pallas_call(
        _out_proj_kernel,
        grid=(n_rows // TM,),
        in_specs=[pl.BlockSpec((TM, half), lambda i: (i, 0)),
                  pl.BlockSpec((TM, half), lambda i: (i, 0)),
                  _layer_resident((half, D_MODEL), w_layer, 0),
                  _layer_resident((half, D_MODEL), w_layer, 1),
                  pl.BlockSpec((TM, D_MODEL), lambda i: (i, 0)),
                  _resident((1, D_MODEL)),
                  _mod_spec(layer, 2, mod_row)],
        out_specs=pl.BlockSpec((TM, D_MODEL), lambda i: (i, 0)),
        out_shape=jax.ShapeDtypeStruct((n_rows, D_MODEL), F32),
        compiler_params=_cparams("parallel"),
        name="out_proj",
    )(y1, y2, w_stack, w_stack, h, g, mods)


def _ffn_kernel(h_ref, g2_ref, sh_ref, sc_ref, wg_ref, wu_ref, wd_ref, g3_ref, gate_ref, o_ref, u_ref, acc_ref):
    j = pl.program_id(1)

    @pl.when(j == 0)
    def _():
        u = _rms(h_ref[...], g2_ref[...]) * (1.0 + sc_ref[...]) + sh_ref[...]
        u_ref[...] = u.astype(BF16)
        acc_ref[...] = jnp.zeros_like(acc_ref)

    u = u_ref[...]
    a = jnp.dot(u, wg_ref[...], preferred_element_type=F32)
    b = jnp.dot(u, wu_ref[...], preferred_element_type=F32)
    hid = (a * jax.nn.sigmoid(a) * b).astype(BF16)
    acc_ref[...] += jnp.dot(hid, wd_ref[...], preferred_element_type=F32)

    @pl.when(j == pl.num_programs(1) - 1)
    def _():
        o_ref[...] = h_ref[...] + gate_ref[...] * _rms(acc_ref[...], g3_ref[...])


def _ffn(h, g2, g3, mods, layer, wg, wu, wd, n_rows, mod_row):
    hidden = wg.shape[2]
    th = TH_FFN
    return pl.pallas_call(
        _ffn_kernel,
        grid=(n_rows // TM, hidden // th),
        in_specs=[pl.BlockSpec((TM, D_MODEL), lambda i, j: (i, 0)),
                  _resident((1, D_MODEL)),
                  _mod_spec(layer, 3, mod_row), _mod_spec(layer, 4, mod_row),
                  pl.BlockSpec((None, D_MODEL, th), lambda i, j: (layer, 0, j)),
                  pl.BlockSpec((None, D_MODEL, th), lambda i, j: (layer, 0, j)),
                  pl.BlockSpec((None, th, D_MODEL), lambda i, j: (layer, j, 0)),
                  _resident((1, D_MODEL)),
                  _mod_spec(layer, 5, mod_row)],
        out_specs=pl.BlockSpec((TM, D_MODEL), lambda i, j: (i, 0)),
        out_shape=jax.ShapeDtypeStruct((n_rows, D_MODEL), F32),
        scratch_shapes=[pltpu.VMEM((TM, D_MODEL), BF16), pltpu.VMEM((TM, D_MODEL), F32)],
        compiler_params=_cparams("parallel", "arbitrary"),
        name="ffn",
    )(h, g2, mods, mods, wg, wu, wd, g3, mods)


def _conv_kernel(v_ref, gt_ref, vp_ref, gp_ref, vn_ref, gn_ref, w_ref, cb_ref, lg_ref, lb_ref, o_ref,
                 buf_ref, sh_ref, acc_ref, *, lat_tiles, tiles_per_seq):
    i = pl.program_id(0)
    tt = v_ref.shape[0]
    pos = i % tiles_per_seq
    is_lat = i < lat_tiles
    has_prev = jnp.logical_and(is_lat, pos != 0)
    has_next = jnp.logical_and(is_lat, pos != tiles_per_seq - 1)

    def glu(v, g):
        return v[...].astype(F32) * jax.nn.sigmoid(g[...].astype(F32))

    buf_ref[0:HALO, :] = jnp.where(has_prev, glu(vp_ref, gp_ref), 0.0)
    buf_ref[HALO:HALO + tt, :] = glu(v_ref, gt_ref)
    buf_ref[HALO + tt:, :] = jnp.where(has_next, glu(vn_ref, gn_ref), 0.0)

    ext = sh_ref.shape[1]
    off = HALO - CONV_K // 2
    for c in range(CONV_CH // LANE):
        lanes = slice(c * LANE, (c + 1) * LANE)
        col = buf_ref[:, lanes]
        for s in range(SUBLANE):
            sh_ref[s, :, lanes] = col[s:s + ext, :]
    for c in range(CONV_CH // LANE):
        lanes = slice(c * LANE, (c + 1) * LANE)
        acc = jnp.broadcast_to(cb_ref[:, lanes], (tt, LANE))
        for k in range(CONV_K):
            s = (k + off) % SUBLANE
            base = (k + off) - s
            acc = acc + w_ref[k:k + 1, lanes] * sh_ref[s, base:base + tt, lanes]
        acc_ref[:, lanes] = acc

    rc = 32
    for r in range(tt // rc):
        x = acc_ref[r * rc:(r + 1) * rc, :]
        mu = jnp.mean(x, axis=-1, keepdims=True)
        xc = x - mu
        var = jnp.mean(xc * xc, axis=-1, keepdims=True)
        y = xc * lax.rsqrt(var + LN_EPS) * lg_ref[...] + lb_ref[...]
        o_ref[r * rc:(r + 1) * rc, :] = (y * jax.nn.sigmoid(y)).astype(o_ref.dtype)


def _conv_branch(p, conv_w, conv_b, ln_g, ln_b, n_lat, seq_lat, seq_ctx, n_rows):
    tt = TT_CONV
    assert seq_ctx == tt and seq_lat % tt == 0
    hb = tt // HALO
    last_hb = n_rows // HALO - 1
    kern = functools.partial(_conv_kernel, lat_tiles=n_lat // tt, tiles_per_seq=seq_lat // tt)

    def prev_map(col):
        return lambda i: (jnp.maximum(i * hb - 1, 0), col)

    def next_map(col):
        return lambda i: (jnp.minimum((i + 1) * hb, last_hb), col)

    return pl.pallas_call(
        kern,
        grid=(n_rows // tt,),
        in_specs=[pl.BlockSpec((tt, CONV_CH), lambda i: (i, 0)),
                  pl.BlockSpec((tt, CONV_CH), lambda i: (i, 1)),
                  pl.BlockSpec((HALO, CONV_CH), prev_map(0)),
                  pl.BlockSpec((HALO, CONV_CH), prev_map(1)),
                  pl.BlockSpec((HALO, CONV_CH), next_map(0)),
                  pl.BlockSpec((HALO, CONV_CH), next_map(1)),
                  _resident((CONV_K, CONV_CH)),
                  _resident((1, CONV_CH)), _resident((1, CONV_CH)), _resident((1, CONV_CH))],
        out_specs=pl.BlockSpec((tt, CONV_CH), lambda i: (i, 0)),
        out_shape=jax.ShapeDtypeStruct((n_rows, CONV_CH), BF16),
        scratch_shapes=[pltpu.VMEM((tt + 2 * HALO, CONV_CH), F32),
                        pltpu.VMEM((SUBLANE, tt + 2 * HALO - SUBLANE, CONV_CH), F32),
                        pltpu.VMEM((tt, CONV_CH), F32)],
        compiler_params=_cparams("parallel"),
        name="conv_branch",
    )(p, p, p, p, p, p, conv_w, conv_b, ln_g, ln_b)


def _dft_ch_kernel(f_ref, cs_ref, zc_ref, zs_ref):
    z = jnp.dot(f_ref[...], cs_ref[...], preferred_element_type=F32)
    zc_ref[...] = z[:, :FOURIER_GC].astype(zc_ref.dtype)
    zs_ref[...] = z[:, FOURIER_GC:].astype(zs_ref.dtype)


def _dft_channels(p, cs_c, n_rows):
    col0 = (2 * CONV_CH) // FOURIER_GC
    out = jax.ShapeDtypeStruct((n_rows, FOURIER_CH), BF16)
    return pl.pallas_call(
        _dft_ch_kernel,
        grid=(n_rows // TM, FOURIER_GROUPS),
        in_specs=[pl.BlockSpec((TM, FOURIER_GC), lambda i, g: (i, col0 + g)),
                  _resident((FOURIER_GC, 2 * FOURIER_GC))],
        out_specs=[pl.BlockSpec((TM, FOURIER_GC), lambda i, g: (i, g)),
                   pl.BlockSpec((TM, FOURIER_GC), lambda i, g: (i, g))],
        out_shape=[out, out],
        compiler_params=_cparams("parallel", "parallel"),
        name="dft_channels",
    )(p, cs_c)


def _dft_pos_kernel(ct_ref, st_ref, zc_ref, zs_ref, ctx_ct_ref, ctx_st_ref, zcx_ref, zsx_ref, o_ref, *, lat_steps):
    def run(c_ref, s_ref, a_ref, b_ref):
        y = (jnp.dot(c_ref[...], a_ref[...], preferred_element_type=F32)
             + jnp.dot(s_ref[...], b_ref[...], preferred_element_type=F32))
        o_ref[...] = y.astype(o_ref.dtype)

    is_ctx = pl.program_id(1) >= lat_steps
    pl.when(is_ctx)(functools.partial(run, ctx_ct_ref, ctx_st_ref, zcx_ref, zsx_ref))
    pl.when(jnp.logical_not(is_ctx))(functools.partial(run, ct_ref, st_ref, zc_ref, zs_ref))


def _dft_positions(ct, nst, ct_ctx, nst_ctx, zc, zs, nbatch, seq_lat, seq_ctx):
    tk = TK_DFT
    assert seq_ctx == tk and seq_lat % tk == 0
    lat_steps = seq_lat // tk
    ctx_blk0 = nbatch * lat_steps

    def tab_map(b, k):
        return (jnp.minimum(k, lat_steps - 1), 0)

    def out_map(b, k):
        return (jnp.where(k < lat_steps, b * lat_steps + k, ctx_blk0 + b), 0)

    lat_tab = pl.BlockSpec((tk, seq_lat), tab_map)
    lat_z = pl.BlockSpec((seq_lat, FOURIER_CH), lambda b, k: (b, 0))
    ctx_z = pl.BlockSpec((seq_ctx, FOURIER_CH), lambda b, k: (ctx_blk0 + b, 0))
    return pl.pallas_call(
        functools.partial(_dft_pos_kernel, lat_steps=lat_steps),
        grid=(nbatch, lat_steps + 1),
        in_specs=[lat_tab, lat_tab, lat_z, lat_z,
                  _resident((seq_ctx, seq_ctx)), _resident((seq_ctx, seq_ctx)), ctx_z, ctx_z],
        out_specs=pl.BlockSpec((tk, FOURIER_CH), out_map),
        out_shape=jax.ShapeDtypeStruct((nbatch * (seq_lat + seq_ctx), FOURIER_CH), BF16),
        compiler_params=_cparams("parallel", "arbitrary"),
        name="dft_positions",
    )(ct, nst, zc, zs, ct_ctx, nst_ctx, zc, zs)


def _dft_tables(n, scale):
    k = jnp.arange(n, dtype=jnp.int32)
    m = (k[:, None] * k[None, :]) % n
    ang = m.astype(F32) * (2.0 * jnp.pi / n)
    return (jnp.cos(ang) * scale).astype(BF16), (-jnp.sin(ang) * scale).astype(BF16)


def _dft_tables_big(n, scale):
    r = int(round(n ** 0.5))
    assert r * r == n
    t = jnp.arange(n, dtype=jnp.int32)[None, :]
    kk = jnp.arange(r, dtype=jnp.int32)[:, None]
    a_hi = ((kk * r * t) % n).astype(F32) * (2.0 * jnp.pi / n)
    a_lo = ((kk * t) % n).astype(F32) * (2.0 * jnp.pi / n)
    ch, sh, cl, sl = jnp.cos(a_hi)[:, None, :], jnp.sin(a_hi)[:, None, :], jnp.cos(a_lo)[None], jnp.sin(a_lo)[None]
    c = (ch * cl - sh * sl) * scale
    s = (sh * cl + ch * sl) * scale
    return c.reshape(n, n).astype(BF16), (-s).reshape(n, n).astype(BF16)


def _rope(x, cos, sin_signed, first_half):
    swapped = jnp.where(first_half, pltpu.roll(x, LANE - 16, 1), pltpu.roll(x, 16, 1))
    return x * cos + swapped * sin_signed


def _odd_proj_kernel(p_ref, cos_ref, sin_ref, gq_ref, gkv_ref, wuq_ref, wuk_ref, wuv_ref,
                     q_ref, k_ref, v_ref, qs_ref, ks_ref, vs_ref):
    tm = p_ref.shape[0]
    cos = cos_ref[...]
    sin = sin_ref[...]
    lane = lax.broadcasted_iota(jnp.int32, (tm, LANE), 1)
    first_half = (lane % 32) < 16
    rope = functools.partial(_rope, cos=cos, sin_signed=sin, first_half=first_half)

    o_kv = Q_LORA
    o_kr = o_kv + KV_LORA
    o_qs = o_kr + LANE
    o_ks = o_qs + SWA_HEADS * SWA_HD
    o_vs = o_ks + SWA_KV_HEADS * SWA_HD

    nq = _rms(p_ref[:, 0:Q_LORA].astype(F32), gq_ref[...]).astype(BF16)
    nkv = _rms(p_ref[:, o_kv:o_kr].astype(F32), gkv_ref[...]).astype(BF16)
    q = jnp.dot(nq, wuq_ref[...], preferred_element_type=F32) * MLA_QSCALE
    kn = jnp.dot(nkv, wuk_ref[...], preferred_element_type=F32)
    v = jnp.dot(nkv, wuv_ref[...], preferred_element_type=F32)
    kr = rope(p_ref[:, o_kr:o_qs].astype(F32)).astype(k_ref.dtype)
    ones = jnp.ones((tm, LANE), v_ref.dtype)
    for h in range(MLA_HEADS):
        c0 = h * MLA_PAD
        q_ref[:, c0:c0 + LANE] = q[:, c0:c0 + LANE].astype(q_ref.dtype)
        q_ref[:, c0 + LANE:c0 + 2 * LANE] = rope(q[:, c0 + LANE:c0 + 2 * LANE]).astype(q_ref.dtype)
        k_ref[:, c0:c0 + LANE] = kn[:, h * MLA_NOPE:(h + 1) * MLA_NOPE].astype(k_ref.dtype)
        k_ref[:, c0 + LANE:c0 + 2 * LANE] = kr
        v_ref[:, c0:c0 + LANE] = v[:, h * MLA_V:(h + 1) * MLA_V].astype(v_ref.dtype)
        v_ref[:, c0 + LANE:c0 + 2 * LANE] = ones
    for j in range(SWA_HEADS * SWA_HD // LANE):
        x = p_ref[:, o_qs + j * LANE:o_qs + (j + 1) * LANE].astype(F32) * SWA_QSCALE
        qs_ref[:, j * LANE:(j + 1) * LANE] = rope(x).astype(qs_ref.dtype)
    ks_ref[...] = rope(p_ref[:, o_ks:o_vs].astype(F32)).astype(ks_ref.dtype)
    vs_ref[...] = p_ref[:, o_vs:o_vs + LANE]


def _odd_proj(p, cos_t, sin_t, gq, gkv, wuq, wuk, wuv, layer, n_lat, seq_lat, n_rows):
    tm = TM
    lat_tiles = n_lat // tm
    per_seq = seq_lat // tm

    def tab_map(i):
        return (jnp.where(i < lat_tiles, i % per_seq, per_seq), 0)

    def rows(w):
        return pl.BlockSpec((tm, w), lambda i: (i, 0))

    def out(w):
        return jax.ShapeDtypeStruct((n_rows, w), BF16)

    pad_w = MLA_HEADS * MLA_PAD
    return pl.pallas_call(
        _odd_proj_kernel,
        grid=(n_rows // tm,),
        in_specs=[rows(p.shape[1]),
                  pl.BlockSpec((tm, LANE), tab_map), pl.BlockSpec((tm, LANE), tab_map),
                  _resident((1, Q_LORA)), _resident((1, KV_LORA)),
                  _layer_resident(wuq.shape[1:], layer), _layer_resident(wuk.shape[1:], layer),
                  _layer_resident(wuv.shape[1:], layer)],
        out_specs=[rows(pad_w), rows(pad_w), rows(pad_w), rows(SWA_HEADS * SWA_HD), rows(LANE), rows(LANE)],
        out_shape=[out(pad_w), out(pad_w), out(pad_w), out(SWA_HEADS * SWA_HD), out(LANE), out(LANE)],
        compiler_params=_cparams("parallel"),
        name="odd_proj",
    )(p, cos_t, sin_t, gq, gkv, wuq, wuk, wuv)


def _rope_tables(seq, pad_rows):
    quarter = MLA_ROPE // 4
    inv = ROPE_BASE ** (-jnp.arange(quarter, dtype=F32) / quarter)
    t = jnp.arange(seq, dtype=jnp.int32)
    row = (t // GRID_W).astype(F32)[:, None] * inv
    col = (t % GRID_W).astype(F32)[:, None] * inv
    ang = jnp.concatenate([row, row, col, col], axis=-1)
    sign = jnp.concatenate([-jnp.ones((quarter,), F32), jnp.ones((quarter,), F32)] * 2)
    cos = jnp.cos(ang)
    sin = jnp.sin(ang) * sign
    cos = jnp.concatenate([jnp.tile(cos, (1, 2)), jnp.ones((pad_rows, LANE), F32)], axis=0)
    sin = jnp.concatenate([jnp.tile(sin, (1, 2)), jnp.zeros((pad_rows, LANE), F32)], axis=0)
    return cos, sin


def _qk(q, k):
    return lax.dot_general(q, k, (((1,), (1,)), ((), ())), preferred_element_type=F32)


def _softmax_pv(scores, values_ext, extra_logit=None):
    m = functools.reduce(jnp.maximum, [jnp.max(s, axis=-1, keepdims=True) for s in scores])
    if extra_logit is not None:
        m = jnp.maximum(m, extra_logit)
    o = functools.reduce(jnp.add, [jnp.dot(jnp.exp2(s - m).astype(BF16), v, preferred_element_type=F32)
                                   for s, v in zip(scores, values_ext)])
    width = o.shape[1] // 2
    l = o[:, width:]
    if extra_logit is not None:
        l = l + jnp.exp2(extra_logit - m)
    return o[:, :width] / l


def _mla_kernel(q_ref, kl_ref, kc_ref, vl_ref, vc_ref, o_ref, *, heads, lat_q_tiles, ctx_queries):
    def attend(parts):
        for hh in range(heads):
            cols = slice(hh * MLA_PAD, (hh + 1) * MLA_PAD)
            q = q_ref[:, cols]
            o = _softmax_pv([_qk(q, k[:, cols]) for k, _ in parts], [v[:, cols] for _, v in parts])
            o_ref[:, hh * MLA_V:(hh + 1) * MLA_V] = o.astype(o_ref.dtype)

    lat_query = functools.partial(attend, [(kl_ref, vl_ref), (kc_ref, vc_ref)])
    ctx_query = functools.partial(attend, [(kc_ref, vc_ref)])
    if ctx_queries:
        is_ctx = pl.program_id(2) >= lat_q_tiles
        pl.when(is_ctx)(ctx_query)
        pl.when(jnp.logical_not(is_ctx))(lat_query)
    else:
        lat_query()


def _mla_attention(q, k, v, nbatch, seq_lat, seq_ctx, ctx_queries):
    tq = TQ_MLA
    hp = MLA_HEADS_PER_STEP
    n_lat = nbatch * seq_lat
    lat_q_tiles = seq_lat // tq
    ctx_q_tiles = seq_ctx // tq if ctx_queries else 0
    out_rows = n_lat + (nbatch * seq_ctx if ctx_queries else 0)
    ctx_blk0 = n_lat // seq_ctx

    def q_map(b, h, i):
        return (jnp.where(i < lat_q_tiles, b * lat_q_tiles + i,
                          n_lat // tq + b * ctx_q_tiles + (i - lat_q_tiles)), h)

    kern = functools.partial(_mla_kernel, heads=hp, lat_q_tiles=lat_q_tiles, ctx_queries=ctx_queries)
    return pl.pallas_call(
        kern,
        grid=(nbatch, MLA_HEADS // hp, lat_q_tiles + ctx_q_tiles),
        in_specs=[pl.BlockSpec((tq, hp * MLA_PAD), q_map),
                  pl.BlockSpec((seq_lat, hp * MLA_PAD), lambda b, h, i: (b, h)),
                  pl.BlockSpec((seq_ctx, hp * MLA_PAD), lambda b, h, i: (ctx_blk0 + b, h)),
                  pl.BlockSpec((seq_lat, hp * MLA_PAD), lambda b, h, i: (b, h)),
                  pl.BlockSpec((seq_ctx, hp * MLA_PAD), lambda b, h, i: (ctx_blk0 + b, h))],
        out_specs=pl.BlockSpec((tq, hp * MLA_V), q_map),
        out_shape=jax.ShapeDtypeStruct((out_rows, MLA_HEADS * MLA_V), BF16),
        compiler_params=_cparams("parallel", "parallel", "arbitrary"),
        name="mla_attention",
    )(q, k, k, v, v)


def _swa_kernel(sink_ref, q_ref, kp_ref, kc_ref, kn_ref, kx_ref, vp_ref, vc_ref, vn_ref, vx_ref, o_ref,
                *, lat_q_blocks, seq_lat, ctx_queries):
    i = pl.program_id(1)
    lo = lax.broadcasted_iota(jnp.int32, (Q_BLOCK, LANE), 1) < LANE // 2
    zero = jnp.zeros((Q_BLOCK, LANE), q_ref.dtype)
    chunks = [q_ref[:, j * LANE:(j + 1) * LANE] for j in range(SWA_GROUP)]
    qstack = jnp.concatenate([jnp.where(lo, ch, zero) for ch in chunks]
                             + [jnp.where(lo, zero, ch) for ch in chunks], axis=0)
    sink = sink_ref[...]

    def attend(k_all, v_all, bias):
        nk = k_all.shape[0]
        s = _qk(qstack, k_all)
        if bias is not None:
            s = (s.reshape(SWA_HEADS, Q_BLOCK, nk) + bias[None]).reshape(SWA_HEADS * Q_BLOCK, nk)
        v_ext = jnp.concatenate([v_all, jnp.ones_like(v_all)], axis=1)
        o = _softmax_pv([s], [v_ext], extra_logit=sink)
        for j in range(SWA_GROUP):
            pair = jnp.where(lo, o[j * Q_BLOCK:(j + 1) * Q_BLOCK],
                             o[(SWA_GROUP + j) * Q_BLOCK:(SWA_GROUP + j + 1) * Q_BLOCK])
            o_ref[:, j * LANE:(j + 1) * LANE] = pair.astype(o_ref.dtype)

    def lat_query():
        k_all = jnp.concatenate([kp_ref[...], kc_ref[...], kn_ref[...], kx_ref[...]], axis=0)
        v_all = jnp.concatenate([vp_ref[...], vc_ref[...], vn_ref[...], vx_ref[...]], axis=0)
        nk = k_all.shape[0]
        span = 3 * Q_BLOCK
        qpos = i * Q_BLOCK + lax.broadcasted_iota(jnp.int32, (Q_BLOCK, nk), 0)
        col = lax.broadcasted_iota(jnp.int32, (Q_BLOCK, nk), 1)
        kpos = (i - 1) * Q_BLOCK + col
        in_window = (kpos >= 0) & (kpos < seq_lat) & (jnp.abs(qpos - kpos) <= WINDOW)
        attend(k_all, v_all, jnp.where((col >= span) | in_window, 0.0, NEG_BIG))

    def ctx_query():
        attend(kx_ref[...], vx_ref[...], None)

    if ctx_queries:
        is_ctx = i >= lat_q_blocks
        pl.when(is_ctx)(ctx_query)
        pl.when(jnp.logical_not(is_ctx))(lat_query)
    else:
        lat_query()


def _swa_attention(sink_rows, qs, ks, vs, nbatch, seq_lat, seq_ctx, ctx_queries):
    qb = Q_BLOCK
    n_lat = nbatch * seq_lat
    lat_q_blocks = seq_lat // qb
    ctx_q_blocks = seq_ctx // qb if ctx_queries else 0
    out_rows = n_lat + (nbatch * seq_ctx if ctx_queries else 0)
    ctx_blk0 = n_lat // seq_ctx

    def q_map(b, i):
        return (jnp.where(i < lat_q_blocks, b * lat_q_blocks + i,
                          n_lat // qb + b * ctx_q_blocks + (i - lat_q_blocks)), 0)

    def k_map(delta):
        def imap(b, i):
            return (b * lat_q_blocks + jnp.clip(i + delta, 0, lat_q_blocks - 1), 0)
        return imap

    kblk = [pl.BlockSpec((qb, LANE), k_map(d)) for d in (-1, 0, 1)]
    xblk = pl.BlockSpec((seq_ctx, LANE), lambda b, i: (ctx_blk0 + b, 0))
    kern = functools.partial(_swa_kernel, lat_q_blocks=lat_q_blocks, seq_lat=seq_lat, ctx_queries=ctx_queries)
    return pl.pallas_call(
        kern,
        grid=(nbatch, lat_q_blocks + ctx_q_blocks),
        in_specs=[_resident(sink_rows.shape),
                  pl.BlockSpec((qb, SWA_HEADS * SWA_HD), q_map)] + kblk + [xblk] + kblk + [xblk],
        out_specs=pl.BlockSpec((qb, SWA_HEADS * SWA_HD), q_map),
        out_shape=jax.ShapeDtypeStruct((out_rows, SWA_HEADS * SWA_HD), BF16),
        compiler_params=_cparams("parallel", "arbitrary"),
        name="swa_attention",
    )(sink_rows, qs, ks, ks, ks, ks, vs, vs, vs, vs)


def _pair_heads(w, axis):
    shape = w.shape
    w = w.reshape(shape[:axis] + (SWA_KV_HEADS, SWA_GROUP, SWA_HD) + shape[axis + 1:])
    w = jnp.swapaxes(w, axis, axis + 1)
    return w.reshape(shape)


def _odd_in_layout(w):
    o2 = Q_LORA + KV_LORA
    o3 = o2 + MLA_ROPE
    o4 = o3 + SWA_HEADS * SWA_HD
    kr = jnp.pad(w[..., o2:o3], ((0, 0), (0, 0), (0, LANE - MLA_ROPE)))
    return jnp.concatenate([w[..., :o2], kr, _pair_heads(w[..., o3:o4], 2), w[..., o4:]], axis=-1).astype(BF16)


def _uq_layout(w):
    per = MLA_NOPE + MLA_ROPE
    w = w.reshape(w.shape[:2] + (MLA_HEADS, per))
    w = jnp.pad(w, ((0, 0), (0, 0), (0, 0), (0, MLA_PAD - per)))
    return w.reshape(w.shape[:2] + (MLA_HEADS * MLA_PAD,)).astype(BF16)


def _out_odd_layout(w):
    half = w.shape[1] // 2
    return jnp.concatenate([w[:, :half], _pair_heads(w[:, half:], 1)], axis=1).astype(BF16)


def kernel(x, c, ctx, c_ctx, w_ada, b_ada, norm_g, w_in_even, conv_w, conv_b, conv_ln_g, conv_ln_b,
           w_in_odd, q_norm_g, kv_norm_g, w_uq, w_uk, w_uv, sink, w_out, w_gate, w_up, w_down):
    nbatch, seq, d = x.shape
    seq_ctx = ctx.shape[1]
    depth = w_ada.shape[0]
    n_lat = nbatch * seq
    n_all = n_lat + nbatch * seq_ctx
    lat_tiles = n_lat // TM
    tiles_per_batch = seq // TM

    def mod_row(i):
        return jnp.where(i < lat_tiles, i // tiles_per_batch, nbatch)

    cvec = jnp.concatenate([c, c_ctx[None, :], jnp.zeros((MOD_ROWS - nbatch - 1, d), F32)], axis=0)
    mods = _ada_mods(cvec, w_ada, b_ada).reshape(depth * MOD_ROWS * 6, 1, d)

    h = jnp.concatenate([x.reshape(n_lat, d), ctx.reshape(nbatch * seq_ctx, d)], axis=0)

    w_even = w_in_even.astype(BF16)
    w_odd = _odd_in_layout(w_in_odd)
    wuq, wuk, wuv = _uq_layout(w_uq), w_uk.astype(BF16), w_uv.astype(BF16)
    wo_even = w_out[0::2].astype(BF16)
    wo_odd = _out_odd_layout(w_out[1::2])
    wg, wu, wd = w_gate.astype(BF16), w_up.astype(BF16), w_down.astype(BF16)
    sink_rows = jnp.repeat(sink * LOG2E, Q_BLOCK, axis=1)[:, :, None]

    ortho = 1.0 / float((seq * FOURIER_GC) ** 0.5)
    ortho_ctx = 1.0 / float((seq_ctx * FOURIER_GC) ** 0.5)
    ct_lat, nst_lat = _dft_tables_big(seq, ortho)
    ct_ctx, nst_ctx = _dft_tables(seq_ctx, ortho_ctx)
    cc, ncs = _dft_tables(FOURIER_GC, 1.0)
    cs_c = jnp.concatenate([cc, -ncs], axis=1)
    cos_t, sin_t = _rope_tables(seq, TM)

    for l in range(depth):
        last = l == depth - 1
        j = l // 2
        g = norm_g[l][:, None, :]
        rows_out = n_lat if last else n_all
        if l % 2 == 0:
            p = _nm_matmul(h, g[0], mods, l, w_even, j, n_all, mod_row)
            y1 = _conv_branch(p, conv_w[j], conv_b[j][None], conv_ln_g[j][None], conv_ln_b[j][None],
                              n_lat, seq, seq_ctx, n_all)
            zc, zs = _dft_channels(p, cs_c, n_all)
            y2 = _dft_positions(ct_lat, nst_lat, ct_ctx, nst_ctx, zc, zs, nbatch, seq, seq_ctx)
            wo = wo_even
        else:
            p = _nm_matmul(h, g[0], mods, l, w_odd, j, n_all, mod_row)
            q, k, v, qs, ks, vs = _odd_proj(p, cos_t, sin_t, q_norm_g[j][None], kv_norm_g[j][None],
                                            wuq, wuk, wuv, j, n_lat, seq, n_all)
            y1 = _mla_attention(q, k, v, nbatch, seq, seq_ctx, not last)
            y2 = _swa_attention(sink_rows[j], qs, ks, vs, nbatch, seq, seq_ctx, not last)
            wo = wo_odd
        h = _out_proj(y1, y2, wo, j, h, g[1], mods, l, rows_out, mod_row)
        h = _ffn(h, g[2], g[3], mods, l, wg, wu, wd, rows_out, mod_row)
    return h.reshape(nbatch, seq, d)
```

```python
import functools
import math

import jax
import jax.numpy as jnp
from jax import lax
from jax.experimental import pallas as pl
from jax.experimental.pallas import tpu as pltpu

F32 = jnp.float32
BF16 = jnp.bfloat16

D_MODEL = 2048
GRID_W = 64
CONV_CH = 1024
CONV_K = 31
FOURIER_CH = 1024
FOURIER_GROUPS = 4
FOURIER_GC = FOURIER_CH // FOURIER_GROUPS
MLA_NOPE = 128
MLA_ROPE = 64
MLA_V = 128
MLA_HEADS = 8
MLA_PAD = 256
Q_LORA = 512
KV_LORA = 512
SWA_HD = 64
SWA_HEADS = 16
SWA_KV_HEADS = 2
SWA_GROUP = SWA_HEADS // SWA_KV_HEADS
WINDOW = 128
Q_BLOCK = 128
ROPE_BASE = 10000.0
NORM_EPS = 1e-6
LN_EPS = 1e-5
LOG2E = math.log2(math.e)
MLA_QSCALE = (MLA_NOPE + MLA_ROPE) ** -0.5 * LOG2E
SWA_QSCALE = SWA_HD ** -0.5 * LOG2E
LANE = 128
SUBLANE = 8
HALO = 16
MOD_ROWS = 8
VMEM_LIMIT = 56 * 1024 * 1024

TM = 512
TM_FFN = 512
TH_FFN = 512
TQ_MLA = 256
TT_CONV = 256
TK_DFT = 256
NEG_BIG = -1e30


def _cparams(*sem):
    return pltpu.CompilerParams(dimension_semantics=sem, vmem_limit_bytes=VMEM_LIMIT)


def _resident(shape):
    nd = len(shape)
    return pl.BlockSpec(shape, lambda *_: (0,) * nd, pipeline_mode=pl.Buffered(1))


def _layer_resident(shape, layer, block=0):
    return pl.BlockSpec((None,) + tuple(shape), lambda *_: (layer, block, 0), pipeline_mode=pl.Buffered(1))


def _rms(x, g):
    return x * lax.rsqrt(jnp.mean(x * x, axis=-1, keepdims=True) + NORM_EPS) * g


def _mod_spec(layer, chunk, mod_row_of_tile):
    def imap(i, *_):
        return ((layer * MOD_ROWS + mod_row_of_tile(i)) * 6 + chunk, 0, 0)
    return pl.BlockSpec((None, 1, D_MODEL), imap)


def _ada_kernel(c_ref, w_ref, b_ref, o_ref):
    c = c_ref[...]
    a = (c * jax.nn.sigmoid(c)).astype(BF16)
    o_ref[...] = jnp.dot(a, w_ref[...].astype(BF16), preferred_element_type=F32) + b_ref[...]


def _ada_mods(cvec, w_ada, b_ada):
    depth, d, n6 = w_ada.shape
    tn = 1024
    return pl.pallas_call(
        _ada_kernel,
        grid=(depth, n6 // tn),
        in_specs=[pl.BlockSpec((MOD_ROWS, d), lambda l, j: (0, 0)),
                  pl.BlockSpec((None, d, tn), lambda l, j: (l, 0, j)),
                  pl.BlockSpec((None, 1, tn), lambda l, j: (l, 0, j))],
        out_specs=pl.BlockSpec((None, MOD_ROWS, tn), lambda l, j: (l, 0, j)),
        out_shape=jax.ShapeDtypeStruct((depth, MOD_ROWS, n6), F32),
        compiler_params=_cparams("parallel", "parallel"),
        name="ada_mods",
    )(cvec, w_ada, b_ada.reshape(depth, 1, n6))


def _norm_mod(h_ref, g_ref, sh_ref, sc_ref):
    return (_rms(h_ref[...], g_ref[...]) * (1.0 + sc_ref[...]) + sh_ref[...]).astype(BF16)


def _nm_matmul_kernel(h_ref, g_ref, sh_ref, sc_ref, w_ref, o_ref):
    u = _norm_mod(h_ref, g_ref, sh_ref, sc_ref)
    o_ref[...] = jnp.dot(u, w_ref[...], preferred_element_type=F32).astype(o_ref.dtype)


def _nm_matmul_dft_kernel(h_ref, g_ref, sh_ref, sc_ref, w_ref, cs_ref, o_ref, zc_ref, zs_ref):
    u = _norm_mod(h_ref, g_ref, sh_ref, sc_ref)
    y = jnp.dot(u, w_ref[...], preferred_element_type=F32)
    n_conv = o_ref.shape[1]
    o_ref[...] = y[:, :n_conv].astype(o_ref.dtype)
    for grp in range(FOURIER_GROUPS):
        cols = slice(grp * FOURIER_GC, (grp + 1) * FOURIER_GC)
        f = y[:, n_conv + grp * FOURIER_GC:n_conv + (grp + 1) * FOURIER_GC].astype(BF16)
        z = jnp.dot(f, cs_ref[...], preferred_element_type=F32)
        zc_ref[:, cols] = z[:, :FOURIER_GC].astype(zc_ref.dtype)
        zs_ref[:, cols] = z[:, FOURIER_GC:].astype(zs_ref.dtype)


def _nm_matmul(h, g, mods, layer, w_stack, w_layer, n_rows, mod_row, cs_c=None):
    nout = w_stack.shape[2]

    def rows(w):
        return pl.BlockSpec((TM, w), lambda i: (i, 0))

    def out(w):
        return jax.ShapeDtypeStruct((n_rows, w), BF16)

    in_specs = [rows(D_MODEL), _resident((1, D_MODEL)),
                _mod_spec(layer, 0, mod_row), _mod_spec(layer, 1, mod_row),
                _layer_resident((D_MODEL, nout), w_layer)]
    args = [h, g, mods, mods, w_stack]
    if cs_c is None:
        kern, out_specs, out_shape = _nm_matmul_kernel, rows(nout), out(nout)
    else:
        n_conv = nout - FOURIER_CH
        kern = _nm_matmul_dft_kernel
        in_specs.append(_resident(cs_c.shape))
        args.append(cs_c)
        out_specs = [rows(n_conv), rows(FOURIER_CH), rows(FOURIER_CH)]
        out_shape = [out(n_conv), out(FOURIER_CH), out(FOURIER_CH)]
    return pl.pallas_call(
        kern,
        grid=(n_rows // TM,),
        in_specs=in_specs,
        out_specs=out_specs,
        out_shape=out_shape,
        compiler_params=_cparams("parallel"),
        name="nm_matmul",
    )(*args)


def _out_proj_kernel(y1_ref, y2_ref, w1_ref, w2_ref, h_ref, g_ref, gate_ref, o_ref):
    y = (jnp.dot(y1_ref[...], w1_ref[...], preferred_element_type=F32)
         + jnp.dot(y2_ref[...], w2_ref[...], preferred_element_type=F32))
    o_ref[...] = h_ref[...] + gate_ref[...] * _rms(y, g_ref[...])


def _out_proj(y1, y2, w_stack, w_layer, h, g, mods, layer, n_rows, mod_row):
    half = y1.shape[1]
    return pl.pallas_call(
        _out_proj_kernel,
        grid=(n_rows // TM,),
        in_specs=[pl.BlockSpec((TM, half), lambda i: (i, 0)),
                  pl.BlockSpec((TM, half), lambda i: (i, 0)),
                  _layer_resident((half, D_MODEL), w_layer, 0),
                  _layer_resident((half, D_MODEL), w_layer, 1),
                  pl.BlockSpec((TM, D_MODEL), lambda i: (i, 0)),
                  _resident((1, D_MODEL)),
                  _mod_spec(layer, 2, mod_row)],
        out_specs=pl.BlockSpec((TM, D_MODEL), lambda i: (i, 0)),
        out_shape=jax.ShapeDtypeStruct((n_rows, D_MODEL), F32),
        compiler_params=_cparams("parallel"),
        name="out_proj",
    )(y1, y2, w_stack, w_stack, h, g, mods)


def _ffn_kernel(h_ref, g2_ref, sh_ref, sc_ref, wg_ref, wu_ref, wd_ref, g3_ref, gate_ref, o_ref, u_ref):
    j = pl.program_id(1)

    @pl.when(j == 0)
    def _():
        u_ref[...] = _norm_mod(h_ref, g2_ref, sh_ref, sc_ref)
        o_ref[...] = jnp.zeros_like(o_ref)

    u = u_ref[...]
    a = jnp.dot(u, wg_ref[...], preferred_element_type=F32)
    b = jnp.dot(u, wu_ref[...], preferred_element_type=F32)
    hid = (a * jax.nn.sigmoid(a) * b).astype(BF16)
    o_ref[...] += jnp.dot(hid, wd_ref[...], preferred_element_type=F32)

    @pl.when(j == pl.num_programs(1) - 1)
    def _():
        o_ref[...] = h_ref[...] + gate_ref[...] * _rms(o_ref[...], g3_ref[...])


def _ffn(h, g2, g3, mods, layer, wg, wu, wd, n_rows, mod_row):
    hidden = wg.shape[2]
    th = TH_FFN
    tm = TM_FFN
    return pl.pallas_call(
        _ffn_kernel,
        grid=(n_rows // tm, hidden // th),
        in_specs=[pl.BlockSpec((tm, D_MODEL), lambda i, j: (i, 0)),
                  _resident((1, D_MODEL)),
                  _mod_spec(layer, 3, mod_row), _mod_spec(layer, 4, mod_row),
                  pl.BlockSpec((None, D_MODEL, th), lambda i, j: (layer, 0, j)),
                  pl.BlockSpec((None, D_MODEL, th), lambda i, j: (layer, 0, j)),
                  pl.BlockSpec((None, th, D_MODEL), lambda i, j: (layer, j, 0)),
                  _resident((1, D_MODEL)),
                  _mod_spec(layer, 5, mod_row)],
        out_specs=pl.BlockSpec((tm, D_MODEL), lambda i, j: (i, 0)),
        out_shape=jax.ShapeDtypeStruct((n_rows, D_MODEL), F32),
        scratch_shapes=[pltpu.VMEM((tm, D_MODEL), BF16)],
        compiler_params=_cparams("parallel", "arbitrary"),
        name="ffn",
    )(h, g2, mods, mods, wg, wu, wd, g3, mods)


def _conv_kernel(v_ref, gt_ref, vp_ref, gp_ref, vn_ref, gn_ref, w_ref, cb_ref, lg_ref, lb_ref, o_ref,
                 buf_ref, sh_ref, acc_ref, *, lat_tiles, tiles_per_seq):
    i = pl.program_id(0)
    tt = v_ref.shape[0]
    pos = i % tiles_per_seq
    is_lat = i < lat_tiles
    has_prev = jnp.logical_and(is_lat, pos != 0)
    has_next = jnp.logical_and(is_lat, pos != tiles_per_seq - 1)

    def glu(v, g):
        return v[...].astype(F32) * jax.nn.sigmoid(g[...].astype(F32))

    buf_ref[0:HALO, :] = jnp.where(has_prev, glu(vp_ref, gp_ref), 0.0)
    buf_ref[HALO:HALO + tt, :] = glu(v_ref, gt_ref)
    buf_ref[HALO + tt:, :] = jnp.where(has_next, glu(vn_ref, gn_ref), 0.0)

    ext = sh_ref.shape[1]
    off = HALO - CONV_K // 2
    for c in range(CONV_CH // LANE):
        lanes = slice(c * LANE, (c + 1) * LANE)
        col = buf_ref[:, lanes]
        for s in range(SUBLANE):
            sh_ref[s, :, lanes] = col[s:s + ext, :]
    for c in range(CONV_CH // LANE):
        lanes = slice(c * LANE, (c + 1) * LANE)
        acc = jnp.broadcast_to(cb_ref[:, lanes], (tt, LANE))
        for k in range(CONV_K):
            s = (k + off) % SUBLANE
            base = (k + off) - s
            acc = acc + w_ref[k:k + 1, lanes] * sh_ref[s, base:base + tt, lanes]
        acc_ref[:, lanes] = acc

    rc = 32
    for r in range(tt // rc):
        x = acc_ref[r * rc:(r + 1) * rc, :]
        mu = jnp.mean(x, axis=-1, keepdims=True)
        xc = x - mu
        var = jnp.mean(xc * xc, axis=-1, keepdims=True)
        y = xc * lax.rsqrt(var + LN_EPS) * lg_ref[...] + lb_ref[...]
        o_ref[r * rc:(r + 1) * rc, :] = (y * jax.nn.sigmoid(y)).astype(o_ref.dtype)


def _conv_branch(p, conv_w, conv_b, ln_g, ln_b, n_lat, seq_lat, seq_ctx, n_rows):
    tt = TT_CONV
    assert seq_ctx == tt and seq_lat % tt == 0
    hb = tt // HALO
    last_hb = n_rows // HALO - 1
    kern = functools.partial(_conv_kernel, lat_tiles=n_lat // tt, tiles_per_seq=seq_lat // tt)

    def prev_map(col):
        return lambda i: (jnp.maximum(i * hb - 1, 0), col)

    def next_map(col):
        return lambda i: (jnp.minimum((i + 1) * hb, last_hb), col)

    return pl.pallas_call(
        kern,
        grid=(n_rows // tt,),
        in_specs=[pl.BlockSpec((tt, CONV_CH), lambda i: (i, 0)),
                  pl.BlockSpec((tt, CONV_CH), lambda i: (i, 1)),
                  pl.BlockSpec((HALO, CONV_CH), prev_map(0)),
                  pl.BlockSpec((HALO, CONV_CH), prev_map(1)),
                  pl.BlockSpec((HALO, CONV_CH), next_map(0)),
                  pl.BlockSpec((HALO, CONV_CH), next_map(1)),
                  _resident((CONV_K, CONV_CH)),
                  _resident((1, CONV_CH)), _resident((1, CONV_CH)), _resident((1, CONV_CH))],
        out_specs=pl.BlockSpec((tt, CONV_CH), lambda i: (i, 0)),
        out_shape=jax.ShapeDtypeStruct((n_rows, CONV_CH), BF16),
        scratch_shapes=[pltpu.VMEM((tt + 2 * HALO, CONV_CH), F32),
                        pltpu.VMEM((SUBLANE, tt + 2 * HALO - SUBLANE, CONV_CH), F32),
                        pltpu.VMEM((tt, CONV_CH), F32)],
        compiler_params=_cparams("parallel"),
        name="conv_branch",
    )(p, p, p, p, p, p, conv_w, conv_b, ln_g, ln_b)


def _dft_pos_kernel(ct_ref, st_ref, zc_ref, zs_ref, ctx_ct_ref, ctx_st_ref, zcx_ref, zsx_ref, o_ref, *, lat_steps):
    def run(c_ref, s_ref, a_ref, b_ref):
        y = (jnp.dot(c_ref[...], a_ref[...], preferred_element_type=F32)
             + jnp.dot(s_ref[...], b_ref[...], preferred_element_type=F32))
        o_ref[...] = y.astype(o_ref.dtype)

    is_ctx = pl.program_id(1) >= lat_steps
    pl.when(is_ctx)(functools.partial(run, ctx_ct_ref, ctx_st_ref, zcx_ref, zsx_ref))
    pl.when(jnp.logical_not(is_ctx))(functools.partial(run, ct_ref, st_ref, zc_ref, zs_ref))


def _dft_positions(ct, nst, ct_ctx, nst_ctx, zc, zs, nbatch, seq_lat, seq_ctx):
    tk = TK_DFT
    assert seq_ctx == tk and seq_lat % tk == 0
    lat_steps = seq_lat // tk
    ctx_blk0 = nbatch * lat_steps

    def tab_map(b, k):
        return (jnp.minimum(k, lat_steps - 1), 0)

    def out_map(b, k):
        return (jnp.where(k < lat_steps, b * lat_steps + k, ctx_blk0 + b), 0)

    lat_tab = pl.BlockSpec((tk, seq_lat), tab_map)
    lat_z = pl.BlockSpec((seq_lat, FOURIER_CH), lambda b, k: (b, 0))
    ctx_z = pl.BlockSpec((seq_ctx, FOURIER_CH), lambda b, k: (ctx_blk0 + b, 0))
    return pl.pallas_call(
        functools.partial(_dft_pos_kernel, lat_steps=lat_steps),
        grid=(nbatch, lat_steps + 1),
        in_specs=[lat_tab, lat_tab, lat_z, lat_z,
                  _resident((seq_ctx, seq_ctx)), _resident((seq_ctx, seq_ctx)), ctx_z, ctx_z],
        out_specs=pl.BlockSpec((tk, FOURIER_CH), out_map),
        out_shape=jax.ShapeDtypeStruct((nbatch * (seq_lat + seq_ctx), FOURIER_CH), BF16),
        compiler_params=_cparams("parallel", "arbitrary"),
        name="dft_positions",
    )(ct, nst, zc, zs, ct_ctx, nst_ctx, zc, zs)


def _dft_tables(n, scale):
    k = jnp.arange(n, dtype=jnp.int32)
    m = (k[:, None] * k[None, :]) % n
    ang = m.astype(F32) * (2.0 * jnp.pi / n)
    return (jnp.cos(ang) * scale).astype(BF16), (-jnp.sin(ang) * scale).astype(BF16)


def _dft_tables_big(n, scale):
    r = int(round(n ** 0.5))
    assert r * r == n
    t = jnp.arange(n, dtype=jnp.int32)[None, :]
    kk = jnp.arange(r, dtype=jnp.int32)[:, None]
    a_hi = ((kk * r * t) % n).astype(F32) * (2.0 * jnp.pi / n)
    a_lo = ((kk * t) % n).astype(F32) * (2.0 * jnp.pi / n)
    ch, sh, cl, sl = jnp.cos(a_hi)[:, None, :], jnp.sin(a_hi)[:, None, :], jnp.cos(a_lo)[None], jnp.sin(a_lo)[None]
    c = (ch * cl - sh * sl) * scale
    s = (sh * cl + ch * sl) * scale
    return c.reshape(n, n).astype(BF16), (-s).reshape(n, n).astype(BF16)


def _rope(x, cos, sin_signed, first_half):
    swapped = jnp.where(first_half, pltpu.roll(x, LANE - 16, 1), pltpu.roll(x, 16, 1))
    return x * cos + swapped * sin_signed


def _odd_proj_kernel(p_ref, cos_ref, sin_ref, gq_ref, gkv_ref, wuq_ref, wuk_ref, wuv_ref,
                     q_ref, k_ref, v_ref, qs_ref, ks_ref, vs_ref):
    tm = p_ref.shape[0]
    cos = cos_ref[...]
    sin = sin_ref[...]
    lane = lax.broadcasted_iota(jnp.int32, (tm, LANE), 1)
    first_half = (lane % 32) < 16
    rope = functools.partial(_rope, cos=cos, sin_signed=sin, first_half=first_half)

    o_kv = Q_LORA
    o_kr = o_kv + KV_LORA
    o_qs = o_kr + LANE
    o_ks = o_qs + SWA_HEADS * SWA_HD
    o_vs = o_ks + SWA_KV_HEADS * SWA_HD

    nq = _rms(p_ref[:, 0:Q_LORA].astype(F32), gq_ref[...]).astype(BF16)
    nkv = _rms(p_ref[:, o_kv:o_kr].astype(F32), gkv_ref[...]).astype(BF16)
    q = jnp.dot(nq, wuq_ref[...], preferred_element_type=F32) * MLA_QSCALE
    kn = jnp.dot(nkv, wuk_ref[...], preferred_element_type=F32)
    v = jnp.dot(nkv, wuv_ref[...], preferred_element_type=F32)
    kr = rope(p_ref[:, o_kr:o_qs].astype(F32)).astype(k_ref.dtype)
    ones = jnp.ones((tm, LANE), v_ref.dtype)
    for h in range(MLA_HEADS):
        c0 = h * MLA_PAD
        q_ref[:, c0:c0 + LANE] = q[:, c0:c0 + LANE].astype(q_ref.dtype)
        q_ref[:, c0 + LANE:c0 + 2 * LANE] = rope(q[:, c0 + LANE:c0 + 2 * LANE]).astype(q_ref.dtype)
        k_ref[:, c0:c0 + LANE] = kn[:, h * MLA_NOPE:(h + 1) * MLA_NOPE].astype(k_ref.dtype)
        k_ref[:, c0 + LANE:c0 + 2 * LANE] = kr
        v_ref[:, c0:c0 + LANE] = v[:, h * MLA_V:(h + 1) * MLA_V].astype(v_ref.dtype)
        v_ref[:, c0 + LANE:c0 + 2 * LANE] = ones
    for j in range(SWA_HEADS * SWA_HD // LANE):
        x = p_ref[:, o_qs + j * LANE:o_qs + (j + 1) * LANE].astype(F32) * SWA_QSCALE
        qs_ref[:, j * LANE:(j + 1) * LANE] = rope(x).astype(qs_ref.dtype)
    ks_ref[...] = rope(p_ref[:, o_ks:o_vs].astype(F32)).astype(ks_ref.dtype)
    vs_ref[...] = p_ref[:, o_vs:o_vs + LANE]


def _odd_proj(p, cos_t, sin_t, gq, gkv, wuq, wuk, wuv, layer, n_lat, seq_lat, n_rows):
    tm = TM
    lat_tiles = n_lat // tm
    per_seq = seq_lat // tm

    def tab_map(i):
        return (jnp.where(i < lat_tiles, i % per_seq, per_seq), 0)

    def rows(w):
        return pl.BlockSpec((tm, w), lambda i: (i, 0))

    def out(w):
        return jax.ShapeDtypeStruct((n_rows, w), BF16)

    pad_w = MLA_HEADS * MLA_PAD
    return pl.pallas_call(
        _odd_proj_kernel,
        grid=(n_rows // tm,),
        in_specs=[rows(p.shape[1]),
                  pl.BlockSpec((tm, LANE), tab_map), pl.BlockSpec((tm, LANE), tab_map),
                  _resident((1, Q_LORA)), _resident((1, KV_LORA)),
                  _layer_resident(wuq.shape[1:], layer), _layer_resident(wuk.shape[1:], layer),
                  _layer_resident(wuv.shape[1:], layer)],
        out_specs=[rows(pad_w), rows(pad_w), rows(pad_w), rows(SWA_HEADS * SWA_HD), rows(LANE), rows(LANE)],
        out_shape=[out(pad_w), out(pad_w), out(pad_w), out(SWA_HEADS * SWA_HD), out(LANE), out(LANE)],
        compiler_params=_cparams("parallel"),
        name="odd_proj",
    )(p, cos_t, sin_t, gq, gkv, wuq, wuk, wuv)


def _rope_tables(seq, pad_rows):
    quarter = MLA_ROPE // 4
    inv = ROPE_BASE ** (-jnp.arange(quarter, dtype=F32) / quarter)
    t = jnp.arange(seq, dtype=jnp.int32)
    row = (t // GRID_W).astype(F32)[:, None] * inv
    col = (t % GRID_W).astype(F32)[:, None] * inv
    ang = jnp.concatenate([row, row, col, col], axis=-1)
    sign = jnp.concatenate([-jnp.ones((quarter,), F32), jnp.ones((quarter,), F32)] * 2)
    cos = jnp.cos(ang)
    sin = jnp.sin(ang) * sign
    cos = jnp.concatenate([jnp.tile(cos, (1, 2)), jnp.ones((pad_rows, LANE), F32)], axis=0)
    sin = jnp.concatenate([jnp.tile(sin, (1, 2)), jnp.zeros((pad_rows, LANE), F32)], axis=0)
    return cos, sin


def _qk(q, k):
    return lax.dot_general(q, k, (((1,), (1,)), ((), ())), preferred_element_type=F32)


def _pipeline_cases(i, lat_tiles, ctx_tiles, stage_a, stage_b):
    n = lat_tiles + ctx_tiles
    groups = {}
    for step in range(n + 1):
        has_a, has_b = step < n, step >= 1
        key = (has_a, has_a and step >= lat_tiles, has_b, has_b and step - 1 >= lat_tiles, step % 2)
        groups.setdefault(key, []).append(step)
    for (has_a, a_ctx, has_b, b_ctx, slot), steps in groups.items():
        def body(has_a=has_a, a_ctx=a_ctx, has_b=has_b, b_ctx=b_ctx, slot=slot):
            if has_a:
                stage_a(a_ctx, slot)
            if has_b:
                stage_b(b_ctx, 1 - slot)
        pl.when((i >= steps[0]) & (i <= steps[-1]) & (i % 2 == slot))(body)


def _mla_kernel(q_ref, kl_ref, kc_ref, vl_ref, vc_ref, o_ref, s0_ref, m0_ref, s1_ref, m1_ref,
                *, lat_tiles, ctx_tiles):
    i = pl.program_id(2)
    n_lat_keys = kl_ref.shape[0]
    slots = ((s0_ref, m0_ref), (s1_ref, m1_ref))

    def stage_a(tile_is_ctx, slot):
        s_ref, m_ref = slots[slot]
        q = q_ref[...]
        sc = _qk(q, kc_ref[...])
        m = jnp.max(sc, axis=-1, keepdims=True)
        if not tile_is_ctx:
            sl = _qk(q, kl_ref[...])
            m = jnp.maximum(m, jnp.max(sl, axis=-1, keepdims=True))
            s_ref[:, :n_lat_keys] = sl
        s_ref[:, n_lat_keys:] = sc
        m_ref[...] = jnp.broadcast_to(m, m_ref.shape)

    def stage_b(tile_is_ctx, slot):
        s_ref, m_ref = slots[slot]
        m = m_ref[:, 0:1]
        o = jnp.dot(jnp.exp2(s_ref[:, n_lat_keys:] - m).astype(BF16), vc_ref[...], preferred_element_type=F32)
        if not tile_is_ctx:
            o = o + jnp.dot(jnp.exp2(s_ref[:, :n_lat_keys] - m).astype(BF16), vl_ref[...],
                            preferred_element_type=F32)
        o_ref[...] = (o[:, :MLA_V] / o[:, MLA_V:]).astype(o_ref.dtype)

    _pipeline_cases(i, lat_tiles, ctx_tiles, stage_a, stage_b)


def _mla_attention(q, k, v, nbatch, seq_lat, seq_ctx, ctx_queries):
    tq = TQ_MLA
    n_lat = nbatch * seq_lat
    lat_tiles = seq_lat // tq
    ctx_tiles = seq_ctx // tq if ctx_queries else 0
    n_tiles = lat_tiles + ctx_tiles
    out_rows = n_lat + (nbatch * seq_ctx if ctx_queries else 0)
    ctx_blk0 = n_lat // seq_ctx

    def tile_row(b, t):
        return jnp.where(t < lat_tiles, b * lat_tiles + t, n_lat // tq + b * ctx_tiles + (t - lat_tiles))

    kern = functools.partial(_mla_kernel, lat_tiles=lat_tiles, ctx_tiles=ctx_tiles)
    n_keys = seq_lat + seq_ctx
    return pl.pallas_call(
        kern,
        grid=(nbatch, MLA_HEADS, n_tiles + 1),
        in_specs=[pl.BlockSpec((tq, MLA_PAD), lambda b, h, i: (tile_row(b, jnp.minimum(i, n_tiles - 1)), h)),
                  pl.BlockSpec((seq_lat, MLA_PAD), lambda b, h, i: (b, h)),
                  pl.BlockSpec((seq_ctx, MLA_PAD), lambda b, h, i: (ctx_blk0 + b, h)),
                  pl.BlockSpec((seq_lat, MLA_PAD), lambda b, h, i: (b, h)),
                  pl.BlockSpec((seq_ctx, MLA_PAD), lambda b, h, i: (ctx_blk0 + b, h))],
        out_specs=pl.BlockSpec((tq, MLA_V), lambda b, h, i: (tile_row(b, jnp.maximum(i - 1, 0)), h)),
        out_shape=jax.ShapeDtypeStruct((out_rows, MLA_HEADS * MLA_V), BF16),
        scratch_shapes=[pltpu.VMEM((tq, n_keys), F32), pltpu.VMEM((tq, LANE), F32),
                        pltpu.VMEM((tq, n_keys), F32), pltpu.VMEM((tq, LANE), F32)],
        compiler_params=_cparams("parallel", "parallel", "arbitrary"),
        name="mla_attention",
    )(q, k, k, v, v)


def _swa_kernel(sink_ref, q_ref, kp_ref, kc_ref, kn_ref, kx_ref, vp_ref, vc_ref, vn_ref, vx_ref, o_ref,
                s0_ref, m0_ref, s1_ref, m1_ref, *, lat_tiles, ctx_tiles, seq_lat):
    i = pl.program_id(1)
    nx = kx_ref.shape[0]
    lo = lax.broadcasted_iota(jnp.int32, (Q_BLOCK, LANE), 1) < LANE // 2
    slots = ((s0_ref, m0_ref), (s1_ref, m1_ref))

    def stage_a(tile_is_ctx, slot):
        s_ref, m_ref = slots[slot]
        zero = jnp.zeros((Q_BLOCK, LANE), q_ref.dtype)
        chunks = [q_ref[:, j * LANE:(j + 1) * LANE] for j in range(SWA_GROUP)]
        qstack = jnp.concatenate([jnp.where(lo, ch, zero) for ch in chunks]
                                 + [jnp.where(lo, zero, ch) for ch in chunks], axis=0)
        if tile_is_ctx:
            s = _qk(qstack, kx_ref[...])
            bias = None
        else:
            k_all = jnp.concatenate([kx_ref[...], kp_ref[...], kc_ref[...], kn_ref[...]], axis=0)
            nk = k_all.shape[0]
            qpos = i * Q_BLOCK + lax.broadcasted_iota(jnp.int32, (Q_BLOCK, nk), 0)
            col = lax.broadcasted_iota(jnp.int32, (Q_BLOCK, nk), 1)
            kpos = (i - 1) * Q_BLOCK + (col - nx)
            in_window = (kpos >= 0) & (kpos < seq_lat) & (jnp.abs(qpos - kpos) <= WINDOW)
            bias = jnp.where((col < nx) | in_window, 0.0, NEG_BIG)
            s = _qk(qstack, k_all)
        nk = s.shape[1]
        for r in range(SWA_HEADS):
            rows = slice(r * Q_BLOCK, (r + 1) * Q_BLOCK)
            blk = s[rows] if bias is None else s[rows] + bias
            s_ref[rows, :nk] = blk
            m = jnp.broadcast_to(jnp.max(blk, axis=-1, keepdims=True), (Q_BLOCK, LANE))
            m_ref[rows, :] = jnp.maximum(m, sink_ref[rows, :])

    def stage_b(tile_is_ctx, slot):
        s_ref, m_ref = slots[slot]
        m = m_ref[...]
        if tile_is_ctx:
            nk = nx
            v_all = vx_ref[...]
        else:
            nk = s_ref.shape[1]
            v_all = jnp.concatenate([vx_ref[...], vp_ref[...], vc_ref[...], vn_ref[...]], axis=0)
        v_ext = jnp.concatenate([v_all, jnp.ones_like(v_all)], axis=1)
        p = jnp.concatenate([jnp.exp2(s_ref[:, c * LANE:(c + 1) * LANE] - m).astype(BF16)
                             for c in range(nk // LANE)], axis=1)
        o = jnp.dot(p, v_ext, preferred_element_type=F32)
        o = o[:, :LANE] / (o[:, LANE:] + jnp.exp2(sink_ref[...] - m))
        for j in range(SWA_GROUP):
            pair = jnp.where(lo, o[j * Q_BLOCK:(j + 1) * Q_BLOCK],
                             o[(SWA_GROUP + j) * Q_BLOCK:(SWA_GROUP + j + 1) * Q_BLOCK])
            o_ref[:, j * LANE:(j + 1) * LANE] = pair.astype(o_ref.dtype)

    _pipeline_cases(i, lat_tiles, ctx_tiles, stage_a, stage_b)


def _swa_attention(sink_rows, qs, ks, vs, nbatch, seq_lat, seq_ctx, ctx_queries):
    qb = Q_BLOCK
    n_lat = nbatch * seq_lat
    lat_tiles = seq_lat // qb
    ctx_tiles = seq_ctx // qb if ctx_queries else 0
    n_tiles = lat_tiles + ctx_tiles
    out_rows = n_lat + (nbatch * seq_ctx if ctx_queries else 0)
    ctx_blk0 = n_lat // seq_ctx
    n_keys = seq_ctx + 3 * qb

    def tile_row(b, t):
        return jnp.where(t < lat_tiles, b * lat_tiles + t, n_lat // qb + b * ctx_tiles + (t - lat_tiles))

    def window(offset):
        def imap(b, i):
            return (b * lat_tiles + jnp.clip(i + offset, 0, lat_tiles - 1), 0)
        return pl.BlockSpec((qb, LANE), imap)

    xblk = pl.BlockSpec((seq_ctx, LANE), lambda b, i: (ctx_blk0 + b, 0))
    kern = functools.partial(_swa_kernel, lat_tiles=lat_tiles, ctx_tiles=ctx_tiles, seq_lat=seq_lat)
    width = SWA_HEADS * SWA_HD
    return pl.pallas_call(
        kern,
        grid=(nbatch, n_tiles + 1),
        in_specs=[_resident(sink_rows.shape),
                  pl.BlockSpec((qb, width), lambda b, i: (tile_row(b, jnp.minimum(i, n_tiles - 1)), 0)),
                  window(-1), window(0), window(1), xblk,
                  window(-2), window(-1), window(0), xblk],
        out_specs=pl.BlockSpec((qb, width), lambda b, i: (tile_row(b, jnp.maximum(i - 1, 0)), 0)),
        out_shape=jax.ShapeDtypeStruct((out_rows, width), BF16),
        scratch_shapes=[pltpu.VMEM((SWA_HEADS * qb, n_keys), F32), pltpu.VMEM((SWA_HEADS * qb, LANE), F32),
                        pltpu.VMEM((SWA_HEADS * qb, n_keys), F32), pltpu.VMEM((SWA_HEADS * qb, LANE), F32)],
        compiler_params=_cparams("parallel", "arbitrary"),
        name="swa_attention",
    )(sink_rows, qs, ks, ks, ks, ks, vs, vs, vs, vs)


def _pair_heads(w, axis):
    shape = w.shape
    w = w.reshape(shape[:axis] + (SWA_KV_HEADS, SWA_GROUP, SWA_HD) + shape[axis + 1:])
    w = jnp.swapaxes(w, axis, axis + 1)
    return w.reshape(shape)


def _odd_in_layout(w):
    o2 = Q_LORA + KV_LORA
    o3 = o2 + MLA_ROPE
    o4 = o3 + SWA_HEADS * SWA_HD
    kr = jnp.pad(w[..., o2:o3], ((0, 0), (0, 0), (0, LANE - MLA_ROPE)))
    return jnp.concatenate([w[..., :o2], kr, _pair_heads(w[..., o3:o4], 2), w[..., o4:]], axis=-1).astype(BF16)


def _uq_layout(w):
    per = MLA_NOPE + MLA_ROPE
    w = w.reshape(w.shape[:2] + (MLA_HEADS, per))
    w = jnp.pad(w, ((0, 0), (0, 0), (0, 0), (0, MLA_PAD - per)))
    return w.reshape(w.shape[:2] + (MLA_HEADS * MLA_PAD,)).astype(BF16)


def _out_odd_layout(w):
    half = w.shape[1] // 2
    return jnp.concatenate([w[:, :half], _pair_heads(w[:, half:], 1)], axis=1).astype(BF16)


def kernel(x, c, ctx, c_ctx, w_ada, b_ada, norm_g, w_in_even, conv_w, conv_b, conv_ln_g, conv_ln_b,
           w_in_odd, q_norm_g, kv_norm_g, w_uq, w_uk, w_uv, sink, w_out, w_gate, w_up, w_down):
    nbatch, seq, d = x.shape
    seq_ctx = ctx.shape[1]
    depth = w_ada.shape[0]
    n_lat = nbatch * seq
    n_all = n_lat + nbatch * seq_ctx

    def mod_row_for(tm):
        return lambda i: jnp.where(i < n_lat // tm, i // (seq // tm), nbatch)

    mod_row = mod_row_for(TM)

    cvec = jnp.concatenate([c, c_ctx[None, :], jnp.zeros((MOD_ROWS - nbatch - 1, d), F32)], axis=0)
    mods = _ada_mods(cvec, w_ada, b_ada).reshape(depth * MOD_ROWS * 6, 1, d)

    h = jnp.concatenate([x.reshape(n_lat, d), ctx.reshape(nbatch * seq_ctx, d)], axis=0)

    w_even = w_in_even.astype(BF16)
    w_odd = _odd_in_layout(w_in_odd)
    wuq, wuk, wuv = _uq_layout(w_uq), w_uk.astype(BF16), w_uv.astype(BF16)
    wo_even = w_out[0::2].astype(BF16)
    wo_odd = _out_odd_layout(w_out[1::2])
    wg, wu, wd = w_gate.astype(BF16), w_up.astype(BF16), w_down.astype(BF16)
    sink_rows = jnp.repeat(sink * LOG2E, Q_BLOCK, axis=1)[:, :, None]
    sink_rows = jnp.broadcast_to(sink_rows, sink_rows.shape[:2] + (LANE,))

    ortho = 1.0 / float((seq * FOURIER_GC) ** 0.5)
    ortho_ctx = 1.0 / float((seq_ctx * FOURIER_GC) ** 0.5)
    ct_lat, nst_lat = _dft_tables_big(seq, ortho)
    ct_ctx, nst_ctx = _dft_tables(seq_ctx, ortho_ctx)
    cc, ncs = _dft_tables(FOURIER_GC, 1.0)
    cs_c = jnp.concatenate([cc, -ncs], axis=1)
    cos_t, sin_t = _rope_tables(seq, TM)

    for l in range(depth):
        last = l == depth - 1
        j = l // 2
        g = norm_g[l][:, None, :]
        rows_out = n_lat if last else n_all
        if l % 2 == 0:
            p, zc, zs = _nm_matmul(h, g[0], mods, l, w_even, j, n_all, mod_row, cs_c)
            y1 = _conv_branch(p, conv_w[j], conv_b[j][None], conv_ln_g[j][None], conv_ln_b[j][None],
                              n_lat, seq, seq_ctx, n_all)
            y2 = _dft_positions(ct_lat, nst_lat, ct_ctx, nst_ctx, zc, zs, nbatch, seq, seq_ctx)
            wo = wo_even
        else:
            p = _nm_matmul(h, g[0], mods, l, w_odd, j, n_all, mod_row)
            q, k, v, qs, ks, vs = _odd_proj(p, cos_t, sin_t, q_norm_g[j][None], kv_norm_g[j][None],
                                            wuq, wuk, wuv, j, n_lat, seq, n_all)
            y1 = _mla_attention(q, k, v, nbatch, seq, seq_ctx, not last)
            y2 = _swa_attention(sink_rows[j], qs, ks, vs, nbatch, seq, seq_ctx, not last)
            wo = wo_odd
        h = _out_proj(y1, y2, wo, j, h, g[1], mods, l, rows_out, mod_row)
        h = _ffn(h, g[2], g[3], mods, l, wg, wu, wd, rows_out, mod_row_for(TM_FFN))
    return h.reshape(nbatch, seq, d)
```

```python
import functools
import math

import jax
import jax.numpy as jnp
from jax import lax
from jax.experimental import pallas as pl
from jax.experimental.pallas import tpu as pltpu

F32 = jnp.float32
BF16 = jnp.bfloat16

D_MODEL = 2048
GRID_W = 64
CONV_CH = 1024
CONV_K = 31
FOURIER_CH = 1024
FOURIER_GROUPS = 4
FOURIER_GC = FOURIER_CH // FOURIER_GROUPS
MLA_NOPE = 128
MLA_ROPE = 64
MLA_V = 128
MLA_HEADS = 8
MLA_PAD = 256
Q_LORA = 512
KV_LORA = 512
SWA_HD = 64
SWA_HEADS = 16
SWA_KV_HEADS = 2
SWA_GROUP = SWA_HEADS // SWA_KV_HEADS
WINDOW = 128
Q_BLOCK = 128
ROPE_BASE = 10000.0
NORM_EPS = 1e-6
LN_EPS = 1e-5
LOG2E = math.log2(math.e)
MLA_QSCALE = (MLA_NOPE + MLA_ROPE) ** -0.5 * LOG2E
SWA_QSCALE = SWA_HD ** -0.5 * LOG2E
LANE = 128
SUBLANE = 8
HALO = 16
MOD_ROWS = 8
VMEM_LIMIT = 56 * 1024 * 1024

TM = 512
TM_FFN = 512
TH_FFN = 512
FFN_SIDE_ROWS = 48
TQ_MLA = 256
MLA_HEADS_PER_STEP = 2
TT_CONV = 256
TK_DFT = 256
NEG_BIG = -1e30


def _cparams(*sem):
    return pltpu.CompilerParams(dimension_semantics=sem, vmem_limit_bytes=VMEM_LIMIT)


def _resident(shape):
    nd = len(shape)
    return pl.BlockSpec(shape, lambda *_: (0,) * nd, pipeline_mode=pl.Buffered(1))


def _layer_resident(shape, layer, block=0):
    return pl.BlockSpec((None,) + tuple(shape), lambda *_: (layer, block, 0), pipeline_mode=pl.Buffered(1))


def _rms(x, g):
    return x * lax.rsqrt(jnp.mean(x * x, axis=-1, keepdims=True) + NORM_EPS) * g


def _mod_spec(layer, chunk, mod_row_of_tile):
    def imap(i, *_):
        return ((layer * MOD_ROWS + mod_row_of_tile(i)) * 6 + chunk, 0, 0)
    return pl.BlockSpec((None, 1, D_MODEL), imap)


def _ada_kernel(c_ref, w_ref, b_ref, o_ref):
    c = c_ref[...]
    a = (c * jax.nn.sigmoid(c)).astype(BF16)
    o_ref[...] = jnp.dot(a, w_ref[...].astype(BF16), preferred_element_type=F32) + b_ref[...]


def _ada_mods(cvec, w_ada, b_ada):
    depth, d, n6 = w_ada.shape
    tn = 1024
    return pl.pallas_call(
        _ada_kernel,
        grid=(depth, n6 // tn),
        in_specs=[pl.BlockSpec((MOD_ROWS, d), lambda l, j: (0, 0)),
                  pl.BlockSpec((None, d, tn), lambda l, j: (l, 0, j)),
                  pl.BlockSpec((None, 1, tn), lambda l, j: (l, 0, j))],
        out_specs=pl.BlockSpec((None, MOD_ROWS, tn), lambda l, j: (l, 0, j)),
        out_shape=jax.ShapeDtypeStruct((depth, MOD_ROWS, n6), F32),
        compiler_params=_cparams("parallel", "parallel"),
        name="ada_mods",
    )(cvec, w_ada, b_ada.reshape(depth, 1, n6))


def _norm_mod(h_ref, g_ref, sh_ref, sc_ref):
    return (_rms(h_ref[...], g_ref[...]) * (1.0 + sc_ref[...]) + sh_ref[...]).astype(BF16)


def _nm_matmul_kernel(h_ref, g_ref, sh_ref, sc_ref, w_ref, o_ref):
    u = _norm_mod(h_ref, g_ref, sh_ref, sc_ref)
    o_ref[...] = jnp.dot(u, w_ref[...], preferred_element_type=F32).astype(o_ref.dtype)


def _nm_matmul_dft_kernel(h_ref, g_ref, sh_ref, sc_ref, w_ref, cs_ref, o_ref, zc_ref, zs_ref):
    u = _norm_mod(h_ref, g_ref, sh_ref, sc_ref)
    y = jnp.dot(u, w_ref[...], preferred_element_type=F32)
    n_conv = o_ref.shape[1]
    o_ref[...] = y[:, :n_conv].astype(o_ref.dtype)
    for grp in range(FOURIER_GROUPS):
        cols = slice(grp * FOURIER_GC, (grp + 1) * FOURIER_GC)
        f = y[:, n_conv + grp * FOURIER_GC:n_conv + (grp + 1) * FOURIER_GC].astype(BF16)
        z = jnp.dot(f, cs_ref[...], preferred_element_type=F32)
        zc_ref[:, cols] = z[:, :FOURIER_GC].astype(zc_ref.dtype)
        zs_ref[:, cols] = z[:, FOURIER_GC:].astype(zs_ref.dtype)


def _nm_matmul(h, g, mods, layer, w_stack, w_layer, n_rows, mod_row, cs_c=None):
    nout = w_stack.shape[2]

    def rows(w):
        return pl.BlockSpec((TM, w), lambda i: (i, 0))

    def out(w):
        return jax.ShapeDtypeStruct((n_rows, w), BF16)

    in_specs = [rows(D_MODEL), _resident((1, D_MODEL)),
                _mod_spec(layer, 0, mod_row), _mod_spec(layer, 1, mod_row),
                _layer_resident((D_MODEL, nout), w_layer)]
    args = [h, g, mods, mods, w_stack]
    if cs_c is None:
        kern, out_specs, out_shape = _nm_matmul_kernel, rows(nout), out(nout)
    else:
        n_conv = nout - FOURIER_CH
        kern = _nm_matmul_dft_kernel
        in_specs.append(_resident(cs_c.shape))
        args.append(cs_c)
        out_specs = [rows(n_conv), rows(FOURIER_CH), rows(FOURIER_CH)]
        out_shape = [out(n_conv), out(FOURIER_CH), out(FOURIER_CH)]
    return pl.pallas_call(
        kern,
        grid=(n_rows // TM,),
        in_specs=in_specs,
        out_specs=out_specs,
        out_shape=out_shape,
        compiler_params=_cparams("parallel"),
        name="nm_matmul",
    )(*args)


def _out_proj_kernel(y1_ref, y2_ref, w1_ref, w2_ref, h_ref, g_ref, gate_ref, o_ref):
    y = (jnp.dot(y1_ref[...], w1_ref[...], preferred_element_type=F32)
         + jnp.dot(y2_ref[...], w2_ref[...], preferred_element_type=F32))
    o_ref[...] = h_ref[...] + gate_ref[...] * _rms(y, g_ref[...])


def _out_proj(y1, y2, w_stack, w_layer, h, g, mods, layer, n_rows, mod_row):
    half = y1.shape[1]
    return pl.pallas_call(
        _out_proj_kernel,
        grid=(n_rows // TM,),
        in_specs=[pl.BlockSpec((TM, half), lambda i: (i, 0)),
                  pl.BlockSpec((TM, half), lambda i: (i, 0)),
                  _layer_resident((half, D_MODEL), w_layer, 0),
                  _layer_resident((half, D_MODEL), w_layer, 1),
                  pl.BlockSpec((TM, D_MODEL), lambda i: (i, 0)),
                  _resident((1, D_MODEL)),
                  _mod_spec(layer, 2, mod_row)],
        out_specs=pl.BlockSpec((TM, D_MODEL), lambda i: (i, 0)),
        out_shape=jax.ShapeDtypeStruct((n_rows, D_MODEL), F32),
        compiler_params=_cparams("parallel"),
        name="out_proj",
    )(y1, y2, w_stack, w_stack, h, g, mods)


def _ffn_kernel(hn_ref, hp_ref, g2_ref, sh_ref, sc_ref, wg_ref, wu_ref, wd_ref, g3_ref, gate_ref, o_ref,
                u0_ref, acc0_ref, u1_ref, acc1_ref, *, n_tiles):
    i = pl.program_id(0)
    j = pl.program_id(1)
    tm = hn_ref.shape[0]
    rows = pl.ds(pl.multiple_of(jnp.minimum(j * FFN_SIDE_ROWS, tm - FFN_SIDE_ROWS), 16), FFN_SIDE_ROWS)
    slots = ((u0_ref, acc0_ref), (u1_ref, acc1_ref))

    @pl.when((i == 0) & (j == 0))
    def _():
        acc0_ref[...] = jnp.zeros_like(acc0_ref)
        acc1_ref[...] = jnp.zeros_like(acc1_ref)

    def step(parity, with_matmuls):
        u_new, acc_done = slots[parity]
        u_run, acc_run = slots[1 - parity]
        x = hn_ref[rows, :]
        u_new[rows, :] = (_rms(x, g2_ref[...]) * (1.0 + sc_ref[...]) + sh_ref[...]).astype(BF16)
        o_ref[rows, :] = hp_ref[rows, :] + gate_ref[...] * _rms(acc_done[rows, :], g3_ref[...])
        if with_matmuls:
            u = u_run[...]
            a = jnp.dot(u, wg_ref[...], preferred_element_type=F32)
            b = jnp.dot(u, wu_ref[...], preferred_element_type=F32)
            hid = (a * jax.nn.sigmoid(a) * b).astype(BF16)
            c = jnp.dot(hid, wd_ref[...], preferred_element_type=F32)
            acc_run[...] = jnp.where(j == 0, c, acc_run[...] + c)

    running = (i >= 1) & (i <= n_tiles)
    for parity in (0, 1):
        mine = i % 2 == parity
        pl.when(mine & running)(functools.partial(step, parity, True))
        pl.when(mine & jnp.logical_not(running))(functools.partial(step, parity, False))


def _ffn(h, g2, g3, mods, layer, wg, wu, wd, n_rows, mod_row):
    hidden = wg.shape[2]
    th = TH_FFN
    tm = TM_FFN
    n_tiles = n_rows // tm

    def new_tile(i, *_):
        return jnp.minimum(i, n_tiles - 1)

    def done_tile(i, *_):
        return jnp.clip(i - 2, 0, n_tiles - 1)

    n_chunks = hidden // th
    assert n_chunks * FFN_SIDE_ROWS >= tm and (tm - FFN_SIDE_ROWS) % 16 == 0 and FFN_SIDE_ROWS % 16 == 0

    def chunk(i, j):
        return jnp.where(i == 0, 0, jnp.where(i > n_tiles, n_chunks - 1, j))

    return pl.pallas_call(
        functools.partial(_ffn_kernel, n_tiles=n_tiles),
        grid=(n_tiles + 2, hidden // th),
        in_specs=[pl.BlockSpec((tm, D_MODEL), lambda i, j: (new_tile(i), 0)),
                  pl.BlockSpec((tm, D_MODEL), lambda i, j: (done_tile(i), 0)),
                  _resident((1, D_MODEL)),
                  _mod_spec(layer, 3, lambda i: mod_row(new_tile(i))),
                  _mod_spec(layer, 4, lambda i: mod_row(new_tile(i))),
                  pl.BlockSpec((None, D_MODEL, th), lambda i, j: (layer, 0, chunk(i, j))),
                  pl.BlockSpec((None, D_MODEL, th), lambda i, j: (layer, 0, chunk(i, j))),
                  pl.BlockSpec((None, th, D_MODEL), lambda i, j: (layer, chunk(i, j), 0)),
                  _resident((1, D_MODEL)),
                  _mod_spec(layer, 5, lambda i: mod_row(done_tile(i)))],
        out_specs=pl.BlockSpec((tm, D_MODEL), lambda i, j: (done_tile(i), 0)),
        out_shape=jax.ShapeDtypeStruct((n_rows, D_MODEL), F32),
        scratch_shapes=[pltpu.VMEM((tm, D_MODEL), BF16), pltpu.VMEM((tm, D_MODEL), F32),
                        pltpu.VMEM((tm, D_MODEL), BF16), pltpu.VMEM((tm, D_MODEL), F32)],
        compiler_params=_cparams("arbitrary", "arbitrary"),
        name="ffn",
    )(h, h, g2, mods, mods, wg, wu, wd, g3, mods)


def _conv_kernel(v_ref, gt_ref, vp_ref, gp_ref, vn_ref, gn_ref, w_ref, cb_ref, lg_ref, lb_ref, o_ref,
                 buf_ref, sh_ref, acc_ref, *, lat_tiles, tiles_per_seq):
    i = pl.program_id(0)
    tt = v_ref.shape[0]
    pos = i % tiles_per_seq
    is_lat = i < lat_tiles
    has_prev = jnp.logical_and(is_lat, pos != 0)
    has_next = jnp.logical_and(is_lat, pos != tiles_per_seq - 1)

    def glu(v, g):
        return v[...].astype(F32) * jax.nn.sigmoid(g[...].astype(F32))

    buf_ref[0:HALO, :] = jnp.where(has_prev, glu(vp_ref, gp_ref), 0.0)
    buf_ref[HALO:HALO + tt, :] = glu(v_ref, gt_ref)
    buf_ref[HALO + tt:, :] = jnp.where(has_next, glu(vn_ref, gn_ref), 0.0)

    ext = sh_ref.shape[1]
    off = HALO - CONV_K // 2
    for c in range(CONV_CH // LANE):
        lanes = slice(c * LANE, (c + 1) * LANE)
        col = buf_ref[:, lanes]
        for s in range(SUBLANE):
            sh_ref[s, :, lanes] = col[s:s + ext, :]
    for c in range(CONV_CH // LANE):
        lanes = slice(c * LANE, (c + 1) * LANE)
        acc = jnp.broadcast_to(cb_ref[:, lanes], (tt, LANE))
        for k in range(CONV_K):
            s = (k + off) % SUBLANE
            base = (k + off) - s
            acc = acc + w_ref[k:k + 1, lanes] * sh_ref[s, base:base + tt, lanes]
        acc_ref[:, lanes] = acc

    rc = 32
    for r in range(tt // rc):
        x = acc_ref[r * rc:(r + 1) * rc, :]
        mu = jnp.mean(x, axis=-1, keepdims=True)
        xc = x - mu
        var = jnp.mean(xc * xc, axis=-1, keepdims=True)
        y = xc * lax.rsqrt(var + LN_EPS) * lg_ref[...] + lb_ref[...]
        o_ref[r * rc:(r + 1) * rc, :] = (y * jax.nn.sigmoid(y)).astype(o_ref.dtype)


def _conv_branch(p, conv_w, conv_b, ln_g, ln_b, n_lat, seq_lat, seq_ctx, n_rows):
    tt = TT_CONV
    assert seq_ctx == tt and seq_lat % tt == 0
    hb = tt // HALO
    last_hb = n_rows // HALO - 1
    kern = functools.partial(_conv_kernel, lat_tiles=n_lat // tt, tiles_per_seq=seq_lat // tt)

    def prev_map(col):
        return lambda i: (jnp.maximum(i * hb - 1, 0), col)

    def next_map(col):
        return lambda i: (jnp.minimum((i + 1) * hb, last_hb), col)

    return pl.pallas_call(
        kern,
        grid=(n_rows // tt,),
        in_specs=[pl.BlockSpec((tt, CONV_CH), lambda i: (i, 0)),
                  pl.BlockSpec((tt, CONV_CH), lambda i: (i, 1)),
                  pl.BlockSpec((HALO, CONV_CH), prev_map(0)),
                  pl.BlockSpec((HALO, CONV_CH), prev_map(1)),
                  pl.BlockSpec((HALO, CONV_CH), next_map(0)),
                  pl.BlockSpec((HALO, CONV_CH), next_map(1)),
                  _resident((CONV_K, CONV_CH)),
                  _resident((1, CONV_CH)), _resident((1, CONV_CH)), _resident((1, CONV_CH))],
        out_specs=pl.BlockSpec((tt, CONV_CH), lambda i: (i, 0)),
        out_shape=jax.ShapeDtypeStruct((n_rows, CONV_CH), BF16),
        scratch_shapes=[pltpu.VMEM((tt + 2 * HALO, CONV_CH), F32),
                        pltpu.VMEM((SUBLANE, tt + 2 * HALO - SUBLANE, CONV_CH), F32),
                        pltpu.VMEM((tt, CONV_CH), F32)],
        compiler_params=_cparams("parallel"),
        name="conv_branch",
    )(p, p, p, p, p, p, conv_w, conv_b, ln_g, ln_b)


def _dft_pos_kernel(ct_ref, st_ref, zc_ref, zs_ref, ctx_ct_ref, ctx_st_ref, zcx_ref, zsx_ref, o_ref, *, lat_steps):
    def run(c_ref, s_ref, a_ref, b_ref):
        y = (jnp.dot(c_ref[...], a_ref[...], preferred_element_type=F32)
             + jnp.dot(s_ref[...], b_ref[...], preferred_element_type=F32))
        o_ref[...] = y.astype(o_ref.dtype)

    is_ctx = pl.program_id(1) >= lat_steps
    pl.when(is_ctx)(functools.partial(run, ctx_ct_ref, ctx_st_ref, zcx_ref, zsx_ref))
    pl.when(jnp.logical_not(is_ctx))(functools.partial(run, ct_ref, st_ref, zc_ref, zs_ref))


def _dft_positions(ct, nst, ct_ctx, nst_ctx, zc, zs, nbatch, seq_lat, seq_ctx):
    tk = TK_DFT
    assert seq_ctx == tk and seq_lat % tk == 0
    lat_steps = seq_lat // tk
    ctx_blk0 = nbatch * lat_steps

    def tab_map(b, k):
        return (jnp.minimum(k, lat_steps - 1), 0)

    def out_map(b, k):
        return (jnp.where(k < lat_steps, b * lat_steps + k, ctx_blk0 + b), 0)

    lat_tab = pl.BlockSpec((tk, seq_lat), tab_map)
    lat_z = pl.BlockSpec((seq_lat, FOURIER_CH), lambda b, k: (b, 0))
    ctx_z = pl.BlockSpec((seq_ctx, FOURIER_CH), lambda b, k: (ctx_blk0 + b, 0))
    return pl.pallas_call(
        functools.partial(_dft_pos_kernel, lat_steps=lat_steps),
        grid=(nbatch, lat_steps + 1),
        in_specs=[lat_tab, lat_tab, lat_z, lat_z,
                  _resident((seq_ctx, seq_ctx)), _resident((seq_ctx, seq_ctx)), ctx_z, ctx_z],
        out_specs=pl.BlockSpec((tk, FOURIER_CH), out_map),
        out_shape=jax.ShapeDtypeStruct((nbatch * (seq_lat + seq_ctx), FOURIER_CH), BF16),
        compiler_params=_cparams("parallel", "arbitrary"),
        name="dft_positions",
    )(ct, nst, zc, zs, ct_ctx, nst_ctx, zc, zs)


def _dft_tables(n, scale):
    k = jnp.arange(n, dtype=jnp.int32)
    m = (k[:, None] * k[None, :]) % n
    ang = m.astype(F32) * (2.0 * jnp.pi / n)
    return (jnp.cos(ang) * scale).astype(BF16), (-jnp.sin(ang) * scale).astype(BF16)


def _dft_tables_big(n, scale):
    r = int(round(n ** 0.5))
    assert r * r == n
    t = jnp.arange(n, dtype=jnp.int32)[None, :]
    kk = jnp.arange(r, dtype=jnp.int32)[:, None]
    a_hi = ((kk * r * t) % n).astype(F32) * (2.0 * jnp.pi / n)
    a_lo = ((kk * t) % n).astype(F32) * (2.0 * jnp.pi / n)
    ch, sh, cl, sl = jnp.cos(a_hi)[:, None, :], jnp.sin(a_hi)[:, None, :], jnp.cos(a_lo)[None], jnp.sin(a_lo)[None]
    c = (ch * cl - sh * sl) * scale
    s = (sh * cl + ch * sl) * scale
    return c.reshape(n, n).astype(BF16), (-s).reshape(n, n).astype(BF16)


def _rope(x, cos, sin_signed, first_half):
    swapped = jnp.where(first_half, pltpu.roll(x, LANE - 16, 1), pltpu.roll(x, 16, 1))
    return x * cos + swapped * sin_signed


def _odd_proj_kernel(p_ref, cos_ref, sin_ref, gq_ref, gkv_ref, wuq_ref, wuk_ref, wuv_ref,
                     q_ref, k_ref, v_ref, qs_ref, ks_ref, vs_ref):
    tm = p_ref.shape[0]
    cos = cos_ref[...]
    sin = sin_ref[...]
    lane = lax.broadcasted_iota(jnp.int32, (tm, LANE), 1)
    first_half = (lane % 32) < 16
    rope = functools.partial(_rope, cos=cos, sin_signed=sin, first_half=first_half)

    o_kv = Q_LORA
    o_kr = o_kv + KV_LORA
    o_qs = o_kr + LANE
    o_ks = o_qs + SWA_HEADS * SWA_HD
    o_vs = o_ks + SWA_KV_HEADS * SWA_HD

    nq = _rms(p_ref[:, 0:Q_LORA].astype(F32), gq_ref[...]).astype(BF16)
    nkv = _rms(p_ref[:, o_kv:o_kr].astype(F32), gkv_ref[...]).astype(BF16)
    q = jnp.dot(nq, wuq_ref[...], preferred_element_type=F32) * MLA_QSCALE
    kn = jnp.dot(nkv, wuk_ref[...], preferred_element_type=F32)
    v = jnp.dot(nkv, wuv_ref[...], preferred_element_type=F32)
    kr = rope(p_ref[:, o_kr:o_qs].astype(F32)).astype(k_ref.dtype)
    ones = jnp.ones((tm, LANE), v_ref.dtype)
    for h in range(MLA_HEADS):
        c0 = h * MLA_PAD
        q_ref[:, c0:c0 + LANE] = q[:, c0:c0 + LANE].astype(q_ref.dtype)
        q_ref[:, c0 + LANE:c0 + 2 * LANE] = rope(q[:, c0 + LANE:c0 + 2 * LANE]).astype(q_ref.dtype)
        k_ref[:, c0:c0 + LANE] = kn[:, h * MLA_NOPE:(h + 1) * MLA_NOPE].astype(k_ref.dtype)
        k_ref[:, c0 + LANE:c0 + 2 * LANE] = kr
        v_ref[:, c0:c0 + LANE] = v[:, h * MLA_V:(h + 1) * MLA_V].astype(v_ref.dtype)
        v_ref[:, c0 + LANE:c0 + 2 * LANE] = ones
    for j in range(SWA_HEADS * SWA_HD // LANE):
        x = p_ref[:, o_qs + j * LANE:o_qs + (j + 1) * LANE].astype(F32) * SWA_QSCALE
        qs_ref[:, j * LANE:(j + 1) * LANE] = rope(x).astype(qs_ref.dtype)
    ks_ref[...] = rope(p_ref[:, o_ks:o_vs].astype(F32)).astype(ks_ref.dtype)
    vs_ref[...] = p_ref[:, o_vs:o_vs + LANE]


def _odd_proj(p, cos_t, sin_t, gq, gkv, wuq, wuk, wuv, layer, n_lat, seq_lat, n_rows):
    tm = TM
    lat_tiles = n_lat // tm
    per_seq = seq_lat // tm

    def tab_map(i):
        return (jnp.where(i < lat_tiles, i % per_seq, per_seq), 0)

    def rows(w):
        return pl.BlockSpec((tm, w), lambda i: (i, 0))

    def out(w):
        return jax.ShapeDtypeStruct((n_rows, w), BF16)

    pad_w = MLA_HEADS * MLA_PAD
    return pl.pallas_call(
        _odd_proj_kernel,
        grid=(n_rows // tm,),
        in_specs=[rows(p.shape[1]),
                  pl.BlockSpec((tm, LANE), tab_map), pl.BlockSpec((tm, LANE), tab_map),
                  _resident((1, Q_LORA)), _resident((1, KV_LORA)),
                  _layer_resident(wuq.shape[1:], layer), _layer_resident(wuk.shape[1:], layer),
                  _layer_resident(wuv.shape[1:], layer)],
        out_specs=[rows(pad_w), rows(pad_w), rows(pad_w), rows(SWA_HEADS * SWA_HD), rows(LANE), rows(LANE)],
        out_shape=[out(pad_w), out(pad_w), out(pad_w), out(SWA_HEADS * SWA_HD), out(LANE), out(LANE)],
        compiler_params=_cparams("parallel"),
        name="odd_proj",
    )(p, cos_t, sin_t, gq, gkv, wuq, wuk, wuv)


def _rope_tables(seq, pad_rows):
    quarter = MLA_ROPE // 4
    inv = ROPE_BASE ** (-jnp.arange(quarter, dtype=F32) / quarter)
    t = jnp.arange(seq, dtype=jnp.int32)
    row = (t // GRID_W).astype(F32)[:, None] * inv
    col = (t % GRID_W).astype(F32)[:, None] * inv
    ang = jnp.concatenate([row, row, col, col], axis=-1)
    sign = jnp.concatenate([-jnp.ones((quarter,), F32), jnp.ones((quarter,), F32)] * 2)
    cos = jnp.cos(ang)
    sin = jnp.sin(ang) * sign
    cos = jnp.concatenate([jnp.tile(cos, (1, 2)), jnp.ones((pad_rows, LANE), F32)], axis=0)
    sin = jnp.concatenate([jnp.tile(sin, (1, 2)), jnp.zeros((pad_rows, LANE), F32)], axis=0)
    return cos, sin


def _qk(q, k):
    return lax.dot_general(q, k, (((1,), (1,)), ((), ())), preferred_element_type=F32)


def _pipeline_cases(i, lat_tiles, ctx_tiles, stage_a, stage_b):
    n = lat_tiles + ctx_tiles
    groups = {}
    for step in range(n + 1):
        has_a, has_b = step < n, step >= 1
        key = (has_a, has_a and step >= lat_tiles, has_b, has_b and step - 1 >= lat_tiles, step % 2)
        groups.setdefault(key, []).append(step)
    for (has_a, a_ctx, has_b, b_ctx, slot), steps in groups.items():
        def body(has_a=has_a, a_ctx=a_ctx, has_b=has_b, b_ctx=b_ctx, slot=slot):
            if has_a:
                stage_a(a_ctx, slot)
            if has_b:
                stage_b(b_ctx, 1 - slot)
        pl.when((i >= steps[0]) & (i <= steps[-1]) & (i % 2 == slot))(body)


def _mla_kernel(q_ref, kl_ref, kc_ref, vl_ref, vc_ref, o_ref, *scratch, heads, lat_tiles, ctx_tiles):
    i = pl.program_id(2)
    n_lat_keys = kl_ref.shape[0]

    def buffers(slot, head):
        base = 2 * (slot * heads + head)
        return scratch[base], scratch[base + 1]

    def stage_a(tile_is_ctx, slot):
        for head in range(heads):
            s_ref, m_ref = buffers(slot, head)
            cols = slice(head * MLA_PAD, (head + 1) * MLA_PAD)
            q = q_ref[:, cols]
            sc = _qk(q, kc_ref[:, cols])
            m = jnp.max(sc, axis=-1, keepdims=True)
            if not tile_is_ctx:
                sl = _qk(q, kl_ref[:, cols])
                m = jnp.maximum(m, jnp.max(sl, axis=-1, keepdims=True))
                s_ref[:, :n_lat_keys] = sl
            s_ref[:, n_lat_keys:] = sc
            m_ref[...] = jnp.broadcast_to(m, m_ref.shape)

    def stage_b(tile_is_ctx, slot):
        for head in range(heads):
            s_ref, m_ref = buffers(slot, head)
            cols = slice(head * MLA_PAD, (head + 1) * MLA_PAD)
            m = m_ref[:, 0:1]
            o = jnp.dot(jnp.exp2(s_ref[:, n_lat_keys:] - m).astype(BF16), vc_ref[:, cols],
                        preferred_element_type=F32)
            if not tile_is_ctx:
                o = o + jnp.dot(jnp.exp2(s_ref[:, :n_lat_keys] - m).astype(BF16), vl_ref[:, cols],
                                preferred_element_type=F32)
            o_ref[:, head * MLA_V:(head + 1) * MLA_V] = (o[:, :MLA_V] / o[:, MLA_V:]).astype(o_ref.dtype)

    _pipeline_cases(i, lat_tiles, ctx_tiles, stage_a, stage_b)


def _mla_attention(q, k, v, nbatch, seq_lat, seq_ctx, ctx_queries):
    tq = TQ_MLA
    n_lat = nbatch * seq_lat
    lat_tiles = seq_lat // tq
    ctx_tiles = seq_ctx // tq if ctx_queries else 0
    n_tiles = lat_tiles + ctx_tiles
    out_rows = n_lat + (nbatch * seq_ctx if ctx_queries else 0)
    ctx_blk0 = n_lat // seq_ctx

    def tile_row(b, t):
        return jnp.where(t < lat_tiles, b * lat_tiles + t, n_lat // tq + b * ctx_tiles + (t - lat_tiles))

    hp = MLA_HEADS_PER_STEP
    kern = functools.partial(_mla_kernel, heads=hp, lat_tiles=lat_tiles, ctx_tiles=ctx_tiles)
    n_keys = seq_lat + seq_ctx
    width = hp * MLA_PAD
    return pl.pallas_call(
        kern,
        grid=(nbatch, MLA_HEADS // hp, n_tiles + 1),
        in_specs=[pl.BlockSpec((tq, width), lambda b, h, i: (tile_row(b, jnp.minimum(i, n_tiles - 1)), h)),
                  pl.BlockSpec((seq_lat, width), lambda b, h, i: (b, h)),
                  pl.BlockSpec((seq_ctx, width), lambda b, h, i: (ctx_blk0 + b, h)),
                  pl.BlockSpec((seq_lat, width), lambda b, h, i: (b, h)),
                  pl.BlockSpec((seq_ctx, width), lambda b, h, i: (ctx_blk0 + b, h))],
        out_specs=pl.BlockSpec((tq, hp * MLA_V), lambda b, h, i: (tile_row(b, jnp.maximum(i - 1, 0)), h)),
        out_shape=jax.ShapeDtypeStruct((out_rows, MLA_HEADS * MLA_V), BF16),
        scratch_shapes=[pltpu.VMEM((tq, n_keys), F32), pltpu.VMEM((tq, LANE), F32)] * (2 * hp),
        compiler_params=_cparams("parallel", "parallel", "arbitrary"),
        name="mla_attention",
    )(q, k, k, v, v)


def _swa_kernel(sink_ref, q_ref, kp_ref, kc_ref, kn_ref, kx_ref, vp_ref, vc_ref, vn_ref, vx_ref, o_ref,
                s0_ref, m0_ref, s1_ref, m1_ref, *, lat_tiles, ctx_tiles, seq_lat):
    i = pl.program_id(1)
    nx = kx_ref.shape[0]
    lo = lax.broadcasted_iota(jnp.int32, (Q_BLOCK, LANE), 1) < LANE // 2
    slots = ((s0_ref, m0_ref), (s1_ref, m1_ref))

    def stage_a(tile_is_ctx, slot):
        s_ref, m_ref = slots[slot]
        zero = jnp.zeros((Q_BLOCK, LANE), q_ref.dtype)
        chunks = [q_ref[:, j * LANE:(j + 1) * LANE] for j in range(SWA_GROUP)]
        qstack = jnp.concatenate([jnp.where(lo, ch, zero) for ch in chunks]
                                 + [jnp.where(lo, zero, ch) for ch in chunks], axis=0)
        if tile_is_ctx:
            s = _qk(qstack, kx_ref[...])
            bias = None
        else:
            k_all = jnp.concatenate([kx_ref[...], kp_ref[...], kc_ref[...], kn_ref[...]], axis=0)
            nk = k_all.shape[0]
            qpos = i * Q_BLOCK + lax.broadcasted_iota(jnp.int32, (Q_BLOCK, nk), 0)
            col = lax.broadcasted_iota(jnp.int32, (Q_BLOCK, nk), 1)
            kpos = (i - 1) * Q_BLOCK + (col - nx)
            in_window = (kpos >= 0) & (kpos < seq_lat) & (jnp.abs(qpos - kpos) <= WINDOW)
            bias = jnp.where((col < nx) | in_window, 0.0, NEG_BIG)
            s = _qk(qstack, k_all)
        nk = s.shape[1]
        for r in range(SWA_HEADS):
            rows = slice(r * Q_BLOCK, (r + 1) * Q_BLOCK)
            blk = s[rows] if bias is None else s[rows] + bias
            s_ref[rows, :nk] = blk
            m = jnp.broadcast_to(jnp.max(blk, axis=-1, keepdims=True), (Q_BLOCK, LANE))
            m_ref[rows, :] = jnp.maximum(m, sink_ref[rows, :])

    def stage_b(tile_is_ctx, slot):
        s_ref, m_ref = slots[slot]
        m = m_ref[...]
        if tile_is_ctx:
            nk = nx
            v_all = vx_ref[...]
        else:
            nk = s_ref.shape[1]
            v_all = jnp.concatenate([vx_ref[...], vp_ref[...], vc_ref[...], vn_ref[...]], axis=0)
        v_ext = jnp.concatenate([v_all, jnp.ones_like(v_all)], axis=1)
        p = jnp.concatenate([jnp.exp2(s_ref[:, c * LANE:(c + 1) * LANE] - m).astype(BF16)
                             for c in range(nk // LANE)], axis=1)
        o = jnp.dot(p, v_ext, preferred_element_type=F32)
        o = o[:, :LANE] / (o[:, LANE:] + jnp.exp2(sink_ref[...] - m))
        for j in range(SWA_GROUP):
            pair = jnp.where(lo, o[j * Q_BLOCK:(j + 1) * Q_BLOCK],
                             o[(SWA_GROUP + j) * Q_BLOCK:(SWA_GROUP + j + 1) * Q_BLOCK])
            o_ref[:, j * LANE:(j + 1) * LANE] = pair.astype(o_ref.dtype)

    _pipeline_cases(i, lat_tiles, ctx_tiles, stage_a, stage_b)


def _swa_attention(sink_rows, qs, ks, vs, nbatch, seq_lat, seq_ctx, ctx_queries):
    qb = Q_BLOCK
    n_lat = nbatch * seq_lat
    lat_tiles = seq_lat // qb
    ctx_tiles = seq_ctx // qb if ctx_queries else 0
    n_tiles = lat_tiles + ctx_tiles
    out_rows = n_lat + (nbatch * seq_ctx if ctx_queries else 0)
    ctx_blk0 = n_lat // seq_ctx
    n_keys = seq_ctx + 3 * qb

    def tile_row(b, t):
        return jnp.where(t < lat_tiles, b * lat_tiles + t, n_lat // qb + b * ctx_tiles + (t - lat_tiles))

    def window(offset):
        def imap(b, i):
            return (b * lat_tiles + jnp.clip(i + offset, 0, lat_tiles - 1), 0)
        return pl.BlockSpec((qb, LANE), imap)

    xblk = pl.BlockSpec((seq_ctx, LANE), lambda b, i: (ctx_blk0 + b, 0))
    kern = functools.partial(_swa_kernel, lat_tiles=lat_tiles, ctx_tiles=ctx_tiles, seq_lat=seq_lat)
    width = SWA_HEADS * SWA_HD
    return pl.pallas_call(
        kern,
        grid=(nbatch, n_tiles + 1),
        in_specs=[_resident(sink_rows.shape),
                  pl.BlockSpec((qb, width), lambda b, i: (tile_row(b, jnp.minimum(i, n_tiles - 1)), 0)),
                  window(-1), window(0), window(1), xblk,
                  window(-2), window(-1), window(0), xblk],
        out_specs=pl.BlockSpec((qb, width), lambda b, i: (tile_row(b, jnp.maximum(i - 1, 0)), 0)),
        out_shape=jax.ShapeDtypeStruct((out_rows, width), BF16),
        scratch_shapes=[pltpu.VMEM((SWA_HEADS * qb, n_keys), F32), pltpu.VMEM((SWA_HEADS * qb, LANE), F32),
                        pltpu.VMEM((SWA_HEADS * qb, n_keys), F32), pltpu.VMEM((SWA_HEADS * qb, LANE), F32)],
        compiler_params=_cparams("parallel", "arbitrary"),
        name="swa_attention",
    )(sink_rows, qs, ks, ks, ks, ks, vs, vs, vs, vs)


def _pair_heads(w, axis):
    shape = w.shape
    w = w.reshape(shape[:axis] + (SWA_KV_HEADS, SWA_GROUP, SWA_HD) + shape[axis + 1:])
    w = jnp.swapaxes(w, axis, axis + 1)
    return w.reshape(shape)


def _odd_in_layout(w):
    o2 = Q_LORA + KV_LORA
    o3 = o2 + MLA_ROPE
    o4 = o3 + SWA_HEADS * SWA_HD
    kr = jnp.pad(w[..., o2:o3], ((0, 0), (0, 0), (0, LANE - MLA_ROPE)))
    return jnp.concatenate([w[..., :o2], kr, _pair_heads(w[..., o3:o4], 2), w[..., o4:]], axis=-1).astype(BF16)


def _uq_layout(w):
    per = MLA_NOPE + MLA_ROPE
    w = w.reshape(w.shape[:2] + (MLA_HEADS, per))
    w = jnp.pad(w, ((0, 0), (0, 0), (0, 0), (0, MLA_PAD - per)))
    return w.reshape(w.shape[:2] + (MLA_HEADS * MLA_PAD,)).astype(BF16)


def _out_odd_layout(w):
    half = w.shape[1] // 2
    return jnp.concatenate([w[:, :half], _pair_heads(w[:, half:], 1)], axis=1).astype(BF16)


def kernel(x, c, ctx, c_ctx, w_ada, b_ada, norm_g, w_in_even, conv_w, conv_b, conv_ln_g, conv_ln_b,
           w_in_odd, q_norm_g, kv_norm_g, w_uq, w_uk, w_uv, sink, w_out, w_gate, w_up, w_down):
    nbatch, seq, d = x.shape
    seq_ctx = ctx.shape[1]
    depth = w_ada.shape[0]
    n_lat = nbatch * seq
    n_all = n_lat + nbatch * seq_ctx

    def mod_row_for(tm):
        return lambda i: jnp.where(i < n_lat // tm, i // (seq // tm), nbatch)

    mod_row = mod_row_for(TM)

    cvec = jnp.concatenate([c, c_ctx[None, :], jnp.zeros((MOD_ROWS - nbatch - 1, d), F32)], axis=0)
    mods = _ada_mods(cvec, w_ada, b_ada).reshape(depth * MOD_ROWS * 6, 1, d)

    h = jnp.concatenate([x.reshape(n_lat, d), ctx.reshape(nbatch * seq_ctx, d)], axis=0)

    w_even = w_in_even.astype(BF16)
    w_odd = _odd_in_layout(w_in_odd)
    wuq, wuk, wuv = _uq_layout(w_uq), w_uk.astype(BF16), w_uv.astype(BF16)
    wo_even = w_out[0::2].astype(BF16)
    wo_odd = _out_odd_layout(w_out[1::2])
    wg, wu, wd = w_gate.astype(BF16), w_up.astype(BF16), w_down.astype(BF16)
    sink_rows = jnp.repeat(sink * LOG2E, Q_BLOCK, axis=1)[:, :, None]
    sink_rows = jnp.broadcast_to(sink_rows, sink_rows.shape[:2] + (LANE,))

    ortho = 1.0 / float((seq * FOURIER_GC) ** 0.5)
    ortho_ctx = 1.0 / float((seq_ctx * FOURIER_GC) ** 0.5)
    ct_lat, nst_lat = _dft_tables_big(seq, ortho)
    ct_ctx, nst_ctx = _dft_tables(seq_ctx, ortho_ctx)
    cc, ncs = _dft_tables(FOURIER_GC, 1.0)
    cs_c = jnp.concatenate([cc, -ncs], axis=1)
    cos_t, sin_t = _rope_tables(seq, TM)

    for l in range(depth):
        last = l == depth - 1
        j = l // 2
        g = norm_g[l][:, None, :]
        rows_out = n_lat if last else n_all
        if l % 2 == 0:
            p, zc, zs = _nm_matmul(h, g[0], mods, l, w_even, j, n_all, mod_row, cs_c)
            y1 = _conv_branch(p, conv_w[j], conv_b[j][None], conv_ln_g[j][None], conv_ln_b[j][None],
                              n_lat, seq, seq_ctx, n_all)
            y2 = _dft_positions(ct_lat, nst_lat, ct_ctx, nst_ctx, zc, zs, nbatch, seq, seq_ctx)
            wo = wo_even
        else:
            p = _nm_matmul(h, g[0], mods, l, w_odd, j, n_all, mod_row)
            q, k, v, qs, ks, vs = _odd_proj(p, cos_t, sin_t, q_norm_g[j][None], kv_norm_g[j][None],
                                            wuq, wuk, wuv, j, n_lat, seq, n_all)
            y1 = _mla_attention(q, k, v, nbatch, seq, seq_ctx, not last)
            y2 = _swa_attention(sink_rows[j], qs, ks, vs, nbatch, seq, seq_ctx, not last)
            wo = wo_odd
        h = _out_proj(y1, y2, wo, j, h, g[1], mods, l, rows_out, mod_row)
        h = _ffn(h, g[2], g[3], mods, l, wg, wu, wd, rows_out, mod_row_for(TM_FFN))
    return h.reshape(nbatch, seq, d)
```

```python
import functools
import math

import jax
import jax.numpy as jnp
from jax import lax
from jax.experimental import pallas as pl
from jax.experimental.pallas import tpu as pltpu

F32 = jnp.float32
BF16 = jnp.bfloat16

D_MODEL = 2048
GRID_W = 64
CONV_CH = 1024
CONV_K = 31
FOURIER_CH = 1024
FOURIER_GROUPS = 4
FOURIER_GC = FOURIER_CH // FOURIER_GROUPS
MLA_NOPE = 128
MLA_ROPE = 64
MLA_V = 128
MLA_HEADS = 8
MLA_PAD = 256
Q_LORA = 512
KV_LORA = 512
SWA_HD = 64
SWA_HEADS = 16
SWA_KV_HEADS = 2
SWA_GROUP = SWA_HEADS // SWA_KV_HEADS
WINDOW = 128
Q_BLOCK = 128
ROPE_BASE = 10000.0
NORM_EPS = 1e-6
LN_EPS = 1e-5
LOG2E = math.log2(math.e)
MLA_QSCALE = (MLA_NOPE + MLA_ROPE) ** -0.5 * LOG2E
SWA_QSCALE = SWA_HD ** -0.5 * LOG2E
LANE = 128
SUBLANE = 8
HALO = 16
MOD_ROWS = 8
VMEM_LIMIT = 56 * 1024 * 1024

TM = 512
TM_FFN = 512
TH_FFN = 512
TQ_MLA = 256
MLA_HEADS_PER_STEP = 2
TT_CONV = 256
TK_DFT = 256
NEG_BIG = -1e30


def _cparams(*sem):
    return pltpu.CompilerParams(dimension_semantics=sem, vmem_limit_bytes=VMEM_LIMIT)


def _resident(shape):
    nd = len(shape)
    return pl.BlockSpec(shape, lambda *_: (0,) * nd, pipeline_mode=pl.Buffered(1))


def _layer_resident(shape, layer, block=0):
    return pl.BlockSpec((None,) + tuple(shape), lambda *_: (layer, block, 0), pipeline_mode=pl.Buffered(1))


def _rms_scale(x):
    return lax.rsqrt(jnp.mean(x * x, axis=-1, keepdims=True) + NORM_EPS)


def _rms(x, g):
    return x * _rms_scale(x) * g


def _norm_mod(x, g_ref, sh_ref, sc_ref):
    gain = g_ref[...] * (1.0 + sc_ref[...])
    return (x * _rms_scale(x) * gain + sh_ref[...]).astype(BF16)


def _gated_residual(h, y, g_ref, gate_ref):
    return h + y * _rms_scale(y) * (gate_ref[...] * g_ref[...])


def _h_operands(h, tm, n_lat):
    if isinstance(h, tuple):
        lat_tiles = n_lat // tm
        specs = [pl.BlockSpec((tm, D_MODEL), lambda i: (jnp.minimum(i, lat_tiles - 1), 0)),
                 pl.BlockSpec((tm, D_MODEL), lambda i: (jnp.maximum(i - lat_tiles, 0), 0))]
        return specs, list(h), lat_tiles
    specs = [pl.BlockSpec((tm, D_MODEL), lambda i: (i, 0)), _resident((tm, D_MODEL))]
    return specs, [h, h], None


def _tile_h(hl_ref, hc_ref, ctx_start):
    if ctx_start is None:
        return hl_ref[...]
    return jnp.where(pl.program_id(0) >= ctx_start, hc_ref[...], hl_ref[...])


def _mod_spec(layer, chunk, mod_row_of_tile):
    def imap(i, *_):
        return ((layer * MOD_ROWS + mod_row_of_tile(i)) * 6 + chunk, 0, 0)
    return pl.BlockSpec((None, 1, D_MODEL), imap)


def _ada_kernel(c_ref, w_ref, b_ref, o_ref):
    c = c_ref[...]
    a = (c * jax.nn.sigmoid(c)).astype(BF16)
    o_ref[...] = jnp.dot(a, w_ref[...].astype(BF16), preferred_element_type=F32) + b_ref[...]


def _ada_mods(cvec, w_ada, b_ada):
    depth, d, n6 = w_ada.shape
    tn = 1024
    return pl.pallas_call(
        _ada_kernel,
        grid=(depth, n6 // tn),
        in_specs=[pl.BlockSpec((MOD_ROWS, d), lambda l, j: (0, 0)),
                  pl.BlockSpec((None, d, tn), lambda l, j: (l, 0, j)),
                  pl.BlockSpec((None, 1, tn), lambda l, j: (l, 0, j))],
        out_specs=pl.BlockSpec((None, MOD_ROWS, tn), lambda l, j: (l, 0, j)),
        out_shape=jax.ShapeDtypeStruct((depth, MOD_ROWS, n6), F32),
        compiler_params=_cparams("parallel", "parallel"),
        name="ada_mods",
    )(cvec, w_ada, b_ada.reshape(depth, 1, n6))


def _nm_matmul_kernel(hl_ref, hc_ref, g_ref, sh_ref, sc_ref, w_ref, o_ref, *, ctx_start):
    u = _norm_mod(_tile_h(hl_ref, hc_ref, ctx_start), g_ref, sh_ref, sc_ref)
    o_ref[...] = jnp.dot(u, w_ref[...], preferred_element_type=F32).astype(o_ref.dtype)


def _nm_matmul_dft_kernel(hl_ref, hc_ref, g_ref, sh_ref, sc_ref, w_ref, cs_ref, o_ref, zc_ref, zs_ref,
                          *, ctx_start):
    u = _norm_mod(_tile_h(hl_ref, hc_ref, ctx_start), g_ref, sh_ref, sc_ref)
    y = jnp.dot(u, w_ref[...], preferred_element_type=F32)
    n_conv = o_ref.shape[1]
    o_ref[...] = y[:, :n_conv].astype(o_ref.dtype)
    for grp in range(FOURIER_GROUPS):
        cols = slice(grp * FOURIER_GC, (grp + 1) * FOURIER_GC)
        f = y[:, n_conv + grp * FOURIER_GC:n_conv + (grp + 1) * FOURIER_GC].astype(BF16)
        z = jnp.dot(f, cs_ref[...], preferred_element_type=F32)
        zc_ref[:, cols] = z[:, :FOURIER_GC].astype(zc_ref.dtype)
        zs_ref[:, cols] = z[:, FOURIER_GC:].astype(zs_ref.dtype)


def _nm_matmul(h, g, mods, layer, w_stack, w_layer, n_rows, n_lat, mod_row, cs_c=None):
    nout = w_stack.shape[2]

    def rows(w):
        return pl.BlockSpec((TM, w), lambda i: (i, 0))

    def out(w):
        return jax.ShapeDtypeStruct((n_rows, w), BF16)

    h_specs, h_args, ctx_start = _h_operands(h, TM, n_lat)
    in_specs = h_specs + [_resident((1, D_MODEL)),
                          _mod_spec(layer, 0, mod_row), _mod_spec(layer, 1, mod_row),
                          _layer_resident((D_MODEL, nout), w_layer)]
    args = h_args + [g, mods, mods, w_stack]
    if cs_c is None:
        kern, out_specs, out_shape = _nm_matmul_kernel, rows(nout), out(nout)
    else:
        n_conv = nout - FOURIER_CH
        kern = _nm_matmul_dft_kernel
        in_specs.append(_resident(cs_c.shape))
        args.append(cs_c)
        out_specs = [rows(n_conv), rows(FOURIER_CH), rows(FOURIER_CH)]
        out_shape = [out(n_conv), out(FOURIER_CH), out(FOURIER_CH)]
    return pl.pallas_call(
        functools.partial(kern, ctx_start=ctx_start),
        grid=(n_rows // TM,),
        in_specs=in_specs,
        out_specs=out_specs,
        out_shape=out_shape,
        compiler_params=_cparams("parallel"),
        name="nm_matmul",
    )(*args)


def _out_proj_kernel(y1_ref, y2_ref, w1_ref, w2_ref, hl_ref, hc_ref, g_ref, gate_ref, o_ref, *, ctx_start):
    y = (jnp.dot(y1_ref[...], w1_ref[...], preferred_element_type=F32)
         + jnp.dot(y2_ref[...], w2_ref[...], preferred_element_type=F32))
    o_ref[...] = _gated_residual(_tile_h(hl_ref, hc_ref, ctx_start), y, g_ref, gate_ref)


def _out_proj(y1, y2, w_stack, w_layer, h, g, mods, layer, n_rows, n_lat, mod_row):
    half = y1.shape[1]
    h_specs, h_args, ctx_start = _h_operands(h, TM, n_lat)
    return pl.pallas_call(
        functools.partial(_out_proj_kernel, ctx_start=ctx_start),
        grid=(n_rows // TM,),
        in_specs=[pl.BlockSpec((TM, half), lambda i: (i, 0)),
                  pl.BlockSpec((TM, half), lambda i: (i, 0)),
                  _layer_resident((half, D_MODEL), w_layer, 0),
                  _layer_resident((half, D_MODEL), w_layer, 1)] + h_specs + [
                  _resident((1, D_MODEL)),
                  _mod_spec(layer, 2, mod_row)],
        out_specs=pl.BlockSpec((TM, D_MODEL), lambda i: (i, 0)),
        out_shape=jax.ShapeDtypeStruct((n_rows, D_MODEL), F32),
        compiler_params=_cparams("parallel"),
        name="out_proj",
    )(y1, y2, w_stack, w_stack, *h_args, g, mods)


def _ffn_kernel(h_ref, g2_ref, sh_ref, sc_ref, wg_ref, wu_ref, wd_ref, g3_ref, gate_ref, o_ref, u_ref):
    j = pl.program_id(1)

    @pl.when(j == 0)
    def _():
        u_ref[...] = _norm_mod(h_ref[...], g2_ref, sh_ref, sc_ref)
        o_ref[...] = jnp.zeros_like(o_ref)

    u = u_ref[...]
    a = jnp.dot(u, wg_ref[...], preferred_element_type=F32)
    b = jnp.dot(u, wu_ref[...], preferred_element_type=F32)
    hid = (a * jax.nn.sigmoid(a) * b).astype(BF16)
    o_ref[...] += jnp.dot(hid, wd_ref[...], preferred_element_type=F32)

    @pl.when(j == pl.num_programs(1) - 1)
    def _():
        o_ref[...] = _gated_residual(h_ref[...], o_ref[...], g3_ref, gate_ref)


def _ffn(h, g2, g3, mods, layer, wg, wu, wd, n_rows, mod_row):
    hidden = wg.shape[2]
    th = TH_FFN
    tm = TM_FFN
    return pl.pallas_call(
        _ffn_kernel,
        grid=(n_rows // tm, hidden // th),
        in_specs=[pl.BlockSpec((tm, D_MODEL), lambda i, j: (i, 0)),
                  _resident((1, D_MODEL)),
                  _mod_spec(layer, 3, mod_row), _mod_spec(layer, 4, mod_row),
                  pl.BlockSpec((None, D_MODEL, th), lambda i, j: (layer, 0, j)),
                  pl.BlockSpec((None, D_MODEL, th), lambda i, j: (layer, 0, j)),
                  pl.BlockSpec((None, th, D_MODEL), lambda i, j: (layer, j, 0)),
                  _resident((1, D_MODEL)),
                  _mod_spec(layer, 5, mod_row)],
        out_specs=pl.BlockSpec((tm, D_MODEL), lambda i, j: (i, 0)),
        out_shape=jax.ShapeDtypeStruct((n_rows, D_MODEL), F32),
        scratch_shapes=[pltpu.VMEM((tm, D_MODEL), BF16)],
        compiler_params=_cparams("parallel", "arbitrary"),
        name="ffn",
    )(h, g2, mods, mods, wg, wu, wd, g3, mods)


def _conv_kernel(v_ref, gt_ref, vp_ref, gp_ref, vn_ref, gn_ref, w_ref, cb_ref, lg_ref, lb_ref, o_ref,
                 buf_ref, sh_ref, acc_ref, *, lat_tiles, tiles_per_seq):
    i = pl.program_id(0)
    tt = v_ref.shape[0]
    pos = i % tiles_per_seq
    is_lat = i < lat_tiles
    has_prev = jnp.logical_and(is_lat, pos != 0)
    has_next = jnp.logical_and(is_lat, pos != tiles_per_seq - 1)

    def glu(v, g):
        return v[...].astype(F32) * jax.nn.sigmoid(g[...].astype(F32))

    buf_ref[0:HALO, :] = jnp.where(has_prev, glu(vp_ref, gp_ref), 0.0)
    buf_ref[HALO:HALO + tt, :] = glu(v_ref, gt_ref)
    buf_ref[HALO + tt:, :] = jnp.where(has_next, glu(vn_ref, gn_ref), 0.0)

    ext = sh_ref.shape[1]
    off = HALO - CONV_K // 2
    for c in range(CONV_CH // LANE):
        lanes = slice(c * LANE, (c + 1) * LANE)
        col = buf_ref[:, lanes]
        for s in range(SUBLANE):
            sh_ref[s, :, lanes] = col[s:s + ext, :]
    for c in range(CONV_CH // LANE):
        lanes = slice(c * LANE, (c + 1) * LANE)
        acc = jnp.broadcast_to(cb_ref[:, lanes], (tt, LANE))
        for k in range(CONV_K):
            s = (k + off) % SUBLANE
            base = (k + off) - s
            acc = acc + w_ref[k:k + 1, lanes] * sh_ref[s, base:base + tt, lanes]
        acc_ref[:, lanes] = acc

    rc = 32
    for r in range(tt // rc):
        x = acc_ref[r * rc:(r + 1) * rc, :]
        mu = jnp.mean(x, axis=-1, keepdims=True)
        xc = x - mu
        var = jnp.mean(xc * xc, axis=-1, keepdims=True)
        y = xc * lax.rsqrt(var + LN_EPS) * lg_ref[...] + lb_ref[...]
        o_ref[r * rc:(r + 1) * rc, :] = (y * jax.nn.sigmoid(y)).astype(o_ref.dtype)


def _conv_branch(p, conv_w, conv_b, ln_g, ln_b, n_lat, seq_lat, seq_ctx, n_rows):
    tt = TT_CONV
    assert seq_ctx == tt and seq_lat % tt == 0
    hb = tt // HALO
    last_hb = n_rows // HALO - 1
    kern = functools.partial(_conv_kernel, lat_tiles=n_lat // tt, tiles_per_seq=seq_lat // tt)

    def prev_map(col):
        return lambda i: (jnp.maximum(i * hb - 1, 0), col)

    def next_map(col):
        return lambda i: (jnp.minimum((i + 1) * hb, last_hb), col)

    return pl.pallas_call(
        kern,
        grid=(n_rows // tt,),
        in_specs=[pl.BlockSpec((tt, CONV_CH), lambda i: (i, 0)),
                  pl.BlockSpec((tt, CONV_CH), lambda i: (i, 1)),
                  pl.BlockSpec((HALO, CONV_CH), prev_map(0)),
                  pl.BlockSpec((HALO, CONV_CH), prev_map(1)),
                  pl.BlockSpec((HALO, CONV_CH), next_map(0)),
                  pl.BlockSpec((HALO, CONV_CH), next_map(1)),
                  _resident((CONV_K, CONV_CH)),
                  _resident((1, CONV_CH)), _resident((1, CONV_CH)), _resident((1, CONV_CH))],
        out_specs=pl.BlockSpec((tt, CONV_CH), lambda i: (i, 0)),
        out_shape=jax.ShapeDtypeStruct((n_rows, CONV_CH), BF16),
        scratch_shapes=[pltpu.VMEM((tt + 2 * HALO, CONV_CH), F32),
                        pltpu.VMEM((SUBLANE, tt + 2 * HALO - SUBLANE, CONV_CH), F32),
                        pltpu.VMEM((tt, CONV_CH), F32)],
        compiler_params=_cparams("parallel"),
        name="conv_branch",
    )(p, p, p, p, p, p, conv_w, conv_b, ln_g, ln_b)


def _dft_pos_kernel(ct_ref, st_ref, zc_ref, zs_ref, ctx_ct_ref, ctx_st_ref, zcx_ref, zsx_ref, o_ref, *, lat_steps):
    def run(c_ref, s_ref, a_ref, b_ref):
        y = (jnp.dot(c_ref[...], a_ref[...], preferred_element_type=F32)
             + jnp.dot(s_ref[...], b_ref[...], preferred_element_type=F32))
        o_ref[...] = y.astype(o_ref.dtype)

    is_ctx = pl.program_id(1) >= lat_steps
    pl.when(is_ctx)(functools.partial(run, ctx_ct_ref, ctx_st_ref, zcx_ref, zsx_ref))
    pl.when(jnp.logical_not(is_ctx))(functools.partial(run, ct_ref, st_ref, zc_ref, zs_ref))


def _dft_positions(ct, nst, ct_ctx, nst_ctx, zc, zs, nbatch, seq_lat, seq_ctx):
    tk = TK_DFT
    assert seq_ctx == tk and seq_lat % tk == 0
    lat_steps = seq_lat // tk
    ctx_blk0 = nbatch * lat_steps

    def tab_map(b, k):
        return (jnp.minimum(k, lat_steps - 1), 0)

    def out_map(b, k):
        return (jnp.where(k < lat_steps, b * lat_steps + k, ctx_blk0 + b), 0)

    lat_tab = pl.BlockSpec((tk, seq_lat), tab_map)
    lat_z = pl.BlockSpec((seq_lat, FOURIER_CH), lambda b, k: (b, 0))
    ctx_z = pl.BlockSpec((seq_ctx, FOURIER_CH), lambda b, k: (ctx_blk0 + b, 0))
    return pl.pallas_call(
        functools.partial(_dft_pos_kernel, lat_steps=lat_steps),
        grid=(nbatch, lat_steps + 1),
        in_specs=[lat_tab, lat_tab, lat_z, lat_z,
                  _resident((seq_ctx, seq_ctx)), _resident((seq_ctx, seq_ctx)), ctx_z, ctx_z],
        out_specs=pl.BlockSpec((tk, FOURIER_CH), out_map),
        out_shape=jax.ShapeDtypeStruct((nbatch * (seq_lat + seq_ctx), FOURIER_CH), BF16),
        compiler_params=_cparams("parallel", "arbitrary"),
        name="dft_positions",
    )(ct, nst, zc, zs, ct_ctx, nst_ctx, zc, zs)


def _dft_tables(n, scale):
    k = jnp.arange(n, dtype=jnp.int32)
    m = (k[:, None] * k[None, :]) % n
    ang = m.astype(F32) * (2.0 * jnp.pi / n)
    return (jnp.cos(ang) * scale).astype(BF16), (-jnp.sin(ang) * scale).astype(BF16)


def _dft_tables_big(n, scale):
    r = int(round(n ** 0.5))
    assert r * r == n
    t = jnp.arange(n, dtype=jnp.int32)[None, :]
    kk = jnp.arange(r, dtype=jnp.int32)[:, None]
    a_hi = ((kk * r * t) % n).astype(F32) * (2.0 * jnp.pi / n)
    a_lo = ((kk * t) % n).astype(F32) * (2.0 * jnp.pi / n)
    ch, sh, cl, sl = jnp.cos(a_hi)[:, None, :], jnp.sin(a_hi)[:, None, :], jnp.cos(a_lo)[None], jnp.sin(a_lo)[None]
    c = (ch * cl - sh * sl) * scale
    s = (sh * cl + ch * sl) * scale
    return c.reshape(n, n).astype(BF16), (-s).reshape(n, n).astype(BF16)


def _rope(x, cos, sin_signed, first_half):
    swapped = jnp.where(first_half, pltpu.roll(x, LANE - 16, 1), pltpu.roll(x, 16, 1))
    return x * cos + swapped * sin_signed


def _odd_proj_kernel(p_ref, cos_ref, sin_ref, gq_ref, gkv_ref, wuq_ref, wuk_ref, wuv_ref,
                     q_ref, k_ref, v_ref, qs_ref, ks_ref, vs_ref):
    tm = p_ref.shape[0]
    cos = cos_ref[...]
    sin = sin_ref[...]
    lane = lax.broadcasted_iota(jnp.int32, (tm, LANE), 1)
    first_half = (lane % 32) < 16
    rope = functools.partial(_rope, cos=cos, sin_signed=sin, first_half=first_half)

    o_kv = Q_LORA
    o_kr = o_kv + KV_LORA
    o_qs = o_kr + LANE
    o_ks = o_qs + SWA_HEADS * SWA_HD
    o_vs = o_ks + SWA_KV_HEADS * SWA_HD

    nq = _rms(p_ref[:, 0:Q_LORA].astype(F32), gq_ref[...]).astype(BF16)
    nkv = _rms(p_ref[:, o_kv:o_kr].astype(F32), gkv_ref[...]).astype(BF16)
    q = jnp.dot(nq, wuq_ref[...], preferred_element_type=F32) * MLA_QSCALE
    kn = jnp.dot(nkv, wuk_ref[...], preferred_element_type=F32)
    v = jnp.dot(nkv, wuv_ref[...], preferred_element_type=F32)
    kr = rope(p_ref[:, o_kr:o_qs].astype(F32)).astype(k_ref.dtype)
    ones = jnp.ones((tm, LANE), v_ref.dtype)
    for h in range(MLA_HEADS):
        c0 = h * MLA_PAD
        q_ref[:, c0:c0 + LANE] = q[:, c0:c0 + LANE].astype(q_ref.dtype)
        q_ref[:, c0 + LANE:c0 + 2 * LANE] = rope(q[:, c0 + LANE:c0 + 2 * LANE]).astype(q_ref.dtype)
        k_ref[:, c0:c0 + LANE] = kn[:, h * MLA_NOPE:(h + 1) * MLA_NOPE].astype(k_ref.dtype)
        k_ref[:, c0 + LANE:c0 + 2 * LANE] = kr
        v_ref[:, c0:c0 + LANE] = v[:, h * MLA_V:(h + 1) * MLA_V].astype(v_ref.dtype)
        v_ref[:, c0 + LANE:c0 + 2 * LANE] = ones
    for j in range(SWA_HEADS * SWA_HD // LANE):
        x = p_ref[:, o_qs + j * LANE:o_qs + (j + 1) * LANE].astype(F32) * SWA_QSCALE
        qs_ref[:, j * LANE:(j + 1) * LANE] = rope(x).astype(qs_ref.dtype)
    ks_ref[...] = rope(p_ref[:, o_ks:o_vs].astype(F32)).astype(ks_ref.dtype)
    vs_ref[...] = p_ref[:, o_vs:o_vs + LANE]


def _odd_proj(p, cos_t, sin_t, gq, gkv, wuq, wuk, wuv, layer, n_lat, seq_lat, n_rows):
    tm = TM
    lat_tiles = n_lat // tm
    per_seq = seq_lat // tm

    def tab_map(i):
        return (jnp.where(i < lat_tiles, i % per_seq, per_seq), 0)

    def rows(w):
        return pl.BlockSpec((tm, w), lambda i: (i, 0))

    def out(w):
        return jax.ShapeDtypeStruct((n_rows, w), BF16)

    pad_w = MLA_HEADS * MLA_PAD
    return pl.pallas_call(
        _odd_proj_kernel,
        grid=(n_rows // tm,),
        in_specs=[rows(p.shape[1]),
                  pl.BlockSpec((tm, LANE), tab_map), pl.BlockSpec((tm, LANE), tab_map),
                  _resident((1, Q_LORA)), _resident((1, KV_LORA)),
                  _layer_resident(wuq.shape[1:], layer), _layer_resident(wuk.shape[1:], layer),
                  _layer_resident(wuv.shape[1:], layer)],
        out_specs=[rows(pad_w), rows(pad_w), rows(pad_w), rows(SWA_HEADS * SWA_HD), rows(LANE), rows(LANE)],
        out_shape=[out(pad_w), out(pad_w), out(pad_w), out(SWA_HEADS * SWA_HD), out(LANE), out(LANE)],
        compiler_params=_cparams("parallel"),
        name="odd_proj",
    )(p, cos_t, sin_t, gq, gkv, wuq, wuk, wuv)


def _rope_tables(seq, pad_rows):
    quarter = MLA_ROPE // 4
    inv = ROPE_BASE ** (-jnp.arange(quarter, dtype=F32) / quarter)
    t = jnp.arange(seq, dtype=jnp.int32)
    row = (t // GRID_W).astype(F32)[:, None] * inv
    col = (t % GRID_W).astype(F32)[:, None] * inv
    ang = jnp.concatenate([row, row, col, col], axis=-1)
    sign = jnp.concatenate([-jnp.ones((quarter,), F32), jnp.ones((quarter,), F32)] * 2)
    cos = jnp.cos(ang)
    sin = jnp.sin(ang) * sign
    cos = jnp.concatenate([jnp.tile(cos, (1, 2)), jnp.ones((pad_rows, LANE), F32)], axis=0)
    sin = jnp.concatenate([jnp.tile(sin, (1, 2)), jnp.zeros((pad_rows, LANE), F32)], axis=0)
    return cos, sin


def _qk(q, k):
    return lax.dot_general(q, k, (((1,), (1,)), ((), ())), preferred_element_type=F32)


def _pipeline_cases(i, lat_tiles, ctx_tiles, stage_a, stage_b):
    n = lat_tiles + ctx_tiles
    groups = {}
    for step in range(n + 1):
        has_a, has_b = step < n, step >= 1
        key = (has_a, has_a and step >= lat_tiles, has_b, has_b and step - 1 >= lat_tiles, step % 2)
        groups.setdefault(key, []).append(step)
    for (has_a, a_ctx, has_b, b_ctx, slot), steps in groups.items():
        def body(has_a=has_a, a_ctx=a_ctx, has_b=has_b, b_ctx=b_ctx, slot=slot):
            if has_a:
                stage_a(a_ctx, slot)
            if has_b:
                stage_b(b_ctx, 1 - slot)
        pl.when((i >= steps[0]) & (i <= steps[-1]) & (i % 2 == slot))(body)


def _mla_kernel(q_ref, kl_ref, kc_ref, vl_ref, vc_ref, o_ref, *scratch, heads, lat_tiles, ctx_tiles):
    i = pl.program_id(2)
    n_lat_keys = kl_ref.shape[0]

    def buffers(slot, head):
        base = 2 * (slot * heads + head)
        return scratch[base], scratch[base + 1]

    def stage_a(tile_is_ctx, slot):
        for head in range(heads):
            s_ref, m_ref = buffers(slot, head)
            cols = slice(head * MLA_PAD, (head + 1) * MLA_PAD)
            q = q_ref[:, cols]
            sc = _qk(q, kc_ref[:, cols])
            m = jnp.max(sc, axis=-1, keepdims=True)
            if not tile_is_ctx:
                sl = _qk(q, kl_ref[:, cols])
                m = jnp.maximum(m, jnp.max(sl, axis=-1, keepdims=True))
                s_ref[:, :n_lat_keys] = sl
            s_ref[:, n_lat_keys:] = sc
            m_ref[...] = jnp.broadcast_to(m, m_ref.shape)

    def stage_b(tile_is_ctx, slot):
        for head in range(heads):
            s_ref, m_ref = buffers(slot, head)
            cols = slice(head * MLA_PAD, (head + 1) * MLA_PAD)
            m = m_ref[:, 0:1]
            o = jnp.dot(jnp.exp2(s_ref[:, n_lat_keys:] - m).astype(BF16), vc_ref[:, cols],
                        preferred_element_type=F32)
            if not tile_is_ctx:
                o = o + jnp.dot(jnp.exp2(s_ref[:, :n_lat_keys] - m).astype(BF16), vl_ref[:, cols],
                                preferred_element_type=F32)
            o_ref[:, head * MLA_V:(head + 1) * MLA_V] = (o[:, :MLA_V] / o[:, MLA_V:]).astype(o_ref.dtype)

    _pipeline_cases(i, lat_tiles, ctx_tiles, stage_a, stage_b)


def _mla_attention(q, k, v, nbatch, seq_lat, seq_ctx, ctx_queries):
    tq = TQ_MLA
    n_lat = nbatch * seq_lat
    lat_tiles = seq_lat // tq
    ctx_tiles = seq_ctx // tq if ctx_queries else 0
    n_tiles = lat_tiles + ctx_tiles
    out_rows = n_lat + (nbatch * seq_ctx if ctx_queries else 0)
    ctx_blk0 = n_lat // seq_ctx

    def tile_row(b, t):
        return jnp.where(t < lat_tiles, b * lat_tiles + t, n_lat // tq + b * ctx_tiles + (t - lat_tiles))

    hp = MLA_HEADS_PER_STEP
    kern = functools.partial(_mla_kernel, heads=hp, lat_tiles=lat_tiles, ctx_tiles=ctx_tiles)
    n_keys = seq_lat + seq_ctx
    width = hp * MLA_PAD
    return pl.pallas_call(
        kern,
        grid=(nbatch, MLA_HEADS // hp, n_tiles + 1),
        in_specs=[pl.BlockSpec((tq, width), lambda b, h, i: (tile_row(b, jnp.minimum(i, n_tiles - 1)), h)),
                  pl.BlockSpec((seq_lat, width), lambda b, h, i: (b, h)),
                  pl.BlockSpec((seq_ctx, width), lambda b, h, i: (ctx_blk0 + b, h)),
                  pl.BlockSpec((seq_lat, width), lambda b, h, i: (b, h)),
                  pl.BlockSpec((seq_ctx, width), lambda b, h, i: (ctx_blk0 + b, h))],
        out_specs=pl.BlockSpec((tq, hp * MLA_V), lambda b, h, i: (tile_row(b, jnp.maximum(i - 1, 0)), h)),
        out_shape=jax.ShapeDtypeStruct((out_rows, MLA_HEADS * MLA_V), BF16),
        scratch_shapes=[pltpu.VMEM((tq, n_keys), F32), pltpu.VMEM((tq, LANE), F32)] * (2 * hp),
        compiler_params=_cparams("parallel", "parallel", "arbitrary"),
        name="mla_attention",
    )(q, k, k, v, v)


def _swa_kernel(sink_ref, q_ref, kp_ref, kc_ref, kn_ref, kx_ref, vp_ref, vc_ref, vn_ref, vx_ref, o_ref,
                s0_ref, m0_ref, s1_ref, m1_ref, *, lat_tiles, ctx_tiles, seq_lat):
    i = pl.program_id(1)
    nx = kx_ref.shape[0]
    lo = lax.broadcasted_iota(jnp.int32, (Q_BLOCK, LANE), 1) < LANE // 2
    slots = ((s0_ref, m0_ref), (s1_ref, m1_ref))

    def stage_a(tile_is_ctx, slot):
        s_ref, m_ref = slots[slot]
        zero = jnp.zeros((Q_BLOCK, LANE), q_ref.dtype)
        chunks = [q_ref[:, j * LANE:(j + 1) * LANE] for j in range(SWA_GROUP)]
        qstack = jnp.concatenate([jnp.where(lo, ch, zero) for ch in chunks]
                                 + [jnp.where(lo, zero, ch) for ch in chunks], axis=0)
        if tile_is_ctx:
            s = _qk(qstack, kx_ref[...])
            bias = None
        else:
            k_all = jnp.concatenate([kx_ref[...], kp_ref[...], kc_ref[...], kn_ref[...]], axis=0)
            nk = k_all.shape[0]
            qpos = i * Q_BLOCK + lax.broadcasted_iota(jnp.int32, (Q_BLOCK, nk), 0)
            col = lax.broadcasted_iota(jnp.int32, (Q_BLOCK, nk), 1)
            kpos = (i - 1) * Q_BLOCK + (col - nx)
            in_window = (kpos >= 0) & (kpos < seq_lat) & (jnp.abs(qpos - kpos) <= WINDOW)
            bias = jnp.where((col < nx) | in_window, 0.0, NEG_BIG)
            s = _qk(qstack, k_all)
        nk = s.shape[1]
        for r in range(SWA_HEADS):
            rows = slice(r * Q_BLOCK, (r + 1) * Q_BLOCK)
            blk = s[rows] if bias is None else s[rows] + bias
            s_ref[rows, :nk] = blk
            m = jnp.broadcast_to(jnp.max(blk, axis=-1, keepdims=True), (Q_BLOCK, LANE))
            m_ref[rows, :] = jnp.maximum(m, sink_ref[rows, :])

    def stage_b(tile_is_ctx, slot):
        s_ref, m_ref = slots[slot]
        m = m_ref[...]
        if tile_is_ctx:
            nk = nx
            v_all = vx_ref[...]
        else:
            nk = s_ref.shape[1]
            v_all = jnp.concatenate([vx_ref[...], vp_ref[...], vc_ref[...], vn_ref[...]], axis=0)
        v_ext = jnp.concatenate([v_all, jnp.ones_like(v_all)], axis=1)
        p = jnp.concatenate([jnp.exp2(s_ref[:, c * LANE:(c + 1) * LANE] - m).astype(BF16)
                             for c in range(nk // LANE)], axis=1)
        o = jnp.dot(p, v_ext, preferred_element_type=F32)
        o = o[:, :LANE] / (o[:, LANE:] + jnp.exp2(sink_ref[...] - m))
        for j in range(SWA_GROUP):
            pair = jnp.where(lo, o[j * Q_BLOCK:(j + 1) * Q_BLOCK],
                             o[(SWA_GROUP + j) * Q_BLOCK:(SWA_GROUP + j + 1) * Q_BLOCK])
            o_ref[:, j * LANE:(j + 1) * LANE] = pair.astype(o_ref.dtype)

    _pipeline_cases(i, lat_tiles, ctx_tiles, stage_a, stage_b)


def _swa_attention(sink_rows, qs, ks, vs, nbatch, seq_lat, seq_ctx, ctx_queries):
    qb = Q_BLOCK
    n_lat = nbatch * seq_lat
    lat_tiles = seq_lat // qb
    ctx_tiles = seq_ctx // qb if ctx_queries else 0
    n_tiles = lat_tiles + ctx_tiles
    out_rows = n_lat + (nbatch * seq_ctx if ctx_queries else 0)
    ctx_blk0 = n_lat // seq_ctx
    n_keys = seq_ctx + 3 * qb

    def tile_row(b, t):
        return jnp.where(t < lat_tiles, b * lat_tiles + t, n_lat // qb + b * ctx_tiles + (t - lat_tiles))

    def window(offset):
        def imap(b, i):
            return (b * lat_tiles + jnp.clip(i + offset, 0, lat_tiles - 1), 0)
        return pl.BlockSpec((qb, LANE), imap)

    xblk = pl.BlockSpec((seq_ctx, LANE), lambda b, i: (ctx_blk0 + b, 0))
    kern = functools.partial(_swa_kernel, lat_tiles=lat_tiles, ctx_tiles=ctx_tiles, seq_lat=seq_lat)
    width = SWA_HEADS * SWA_HD
    return pl.pallas_call(
        kern,
        grid=(nbatch, n_tiles + 1),
        in_specs=[_resident(sink_rows.shape),
                  pl.BlockSpec((qb, width), lambda b, i: (tile_row(b, jnp.minimum(i, n_tiles - 1)), 0)),
                  window(-1), window(0), window(1), xblk,
                  window(-2), window(-1), window(0), xblk],
        out_specs=pl.BlockSpec((qb, width), lambda b, i: (tile_row(b, jnp.maximum(i - 1, 0)), 0)),
        out_shape=jax.ShapeDtypeStruct((out_rows, width), BF16),
        scratch_shapes=[pltpu.VMEM((SWA_HEADS * qb, n_keys), F32), pltpu.VMEM((SWA_HEADS * qb, LANE), F32),
                        pltpu.VMEM((SWA_HEADS * qb, n_keys), F32), pltpu.VMEM((SWA_HEADS * qb, LANE), F32)],
        compiler_params=_cparams("parallel", "arbitrary"),
        name="swa_attention",
    )(sink_rows, qs, ks, ks, ks, ks, vs, vs, vs, vs)


def _pair_heads(w, axis):
    shape = w.shape
    w = w.reshape(shape[:axis] + (SWA_KV_HEADS, SWA_GROUP, SWA_HD) + shape[axis + 1:])
    w = jnp.swapaxes(w, axis, axis + 1)
    return w.reshape(shape)


def _odd_in_layout(w):
    o2 = Q_LORA + KV_LORA
    o3 = o2 + MLA_ROPE
    o4 = o3 + SWA_HEADS * SWA_HD
    kr = jnp.pad(w[..., o2:o3], ((0, 0), (0, 0), (0, LANE - MLA_ROPE)))
    return jnp.concatenate([w[..., :o2], kr, _pair_heads(w[..., o3:o4], 2), w[..., o4:]], axis=-1).astype(BF16)


def _uq_layout(w):
    per = MLA_NOPE + MLA_ROPE
    w = w.reshape(w.shape[:2] + (MLA_HEADS, per))
    w = jnp.pad(w, ((0, 0), (0, 0), (0, 0), (0, MLA_PAD - per)))
    return w.reshape(w.shape[:2] + (MLA_HEADS * MLA_PAD,)).astype(BF16)


def _out_odd_layout(w):
    half = w.shape[1] // 2
    return jnp.concatenate([w[:, :half], _pair_heads(w[:, half:], 1)], axis=1).astype(BF16)


def kernel(x, c, ctx, c_ctx, w_ada, b_ada, norm_g, w_in_even, conv_w, conv_b, conv_ln_g, conv_ln_b,
           w_in_odd, q_norm_g, kv_norm_g, w_uq, w_uk, w_uv, sink, w_out, w_gate, w_up, w_down):
    nbatch, seq, d = x.shape
    seq_ctx = ctx.shape[1]
    depth = w_ada.shape[0]
    n_lat = nbatch * seq
    n_all = n_lat + nbatch * seq_ctx

    def mod_row_for(tm):
        return lambda i: jnp.where(i < n_lat // tm, i // (seq // tm), nbatch)

    mod_row = mod_row_for(TM)

    cvec = jnp.concatenate([c, c_ctx[None, :], jnp.zeros((MOD_ROWS - nbatch - 1, d), F32)], axis=0)
    mods = _ada_mods(cvec, w_ada, b_ada).reshape(depth * MOD_ROWS * 6, 1, d)

    h = (x.reshape(n_lat, d), ctx.reshape(nbatch * seq_ctx, d))

    w_even = w_in_even.astype(BF16)
    w_odd = _odd_in_layout(w_in_odd)
    wuq, wuk, wuv = _uq_layout(w_uq), w_uk.astype(BF16), w_uv.astype(BF16)
    wo_even = w_out[0::2].astype(BF16)
    wo_odd = _out_odd_layout(w_out[1::2])
    wg, wu, wd = w_gate.astype(BF16), w_up.astype(BF16), w_down.astype(BF16)
    sink_rows = jnp.repeat(sink * LOG2E, Q_BLOCK, axis=1)[:, :, None]
    sink_rows = jnp.broadcast_to(sink_rows, sink_rows.shape[:2] + (LANE,))

    ortho = 1.0 / float((seq * FOURIER_GC) ** 0.5)
    ortho_ctx = 1.0 / float((seq_ctx * FOURIER_GC) ** 0.5)
    ct_lat, nst_lat = _dft_tables_big(seq, ortho)
    ct_ctx, nst_ctx = _dft_tables(seq_ctx, ortho_ctx)
    cc, ncs = _dft_tables(FOURIER_GC, 1.0)
    cs_c = jnp.concatenate([cc, -ncs], axis=1)
    cos_t, sin_t = _rope_tables(seq, TM)

    for l in range(depth):
        last = l == depth - 1
        j = l // 2
        g = norm_g[l][:, None, :]
        rows_out = n_lat if last else n_all
        if l % 2 == 0:
            p, zc, zs = _nm_matmul(h, g[0], mods, l, w_even, j, n_all, n_lat, mod_row, cs_c)
            y1 = _conv_branch(p, conv_w[j], conv_b[j][None], conv_ln_g[j][None], conv_ln_b[j][None],
                              n_lat, seq, seq_ctx, n_all)
            y2 = _dft_positions(ct_lat, nst_lat, ct_ctx, nst_ctx, zc, zs, nbatch, seq, seq_ctx)
            wo = wo_even
        else:
            p = _nm_matmul(h, g[0], mods, l, w_odd, j, n_all, n_lat, mod_row)
            q, k, v, qs, ks, vs = _odd_proj(p, cos_t, sin_t, q_norm_g[j][None], kv_norm_g[j][None],
                                            wuq, wuk, wuv, j, n_lat, seq, n_all)
            y1 = _mla_attention(q, k, v, nbatch, seq, seq_ctx, not last)
            y2 = _swa_attention(sink_rows[j], qs, ks, vs, nbatch, seq, seq_ctx, not last)
            wo = wo_odd
        h = _out_proj(y1, y2, wo, j, h, g[1], mods, l, rows_out, n_lat, mod_row)
        h = _ffn(h, g[2], g[3], mods, l, wg, wu, wd, rows_out, mod_row_for(TM_FFN))
    return h.reshape(nbatch, seq, d)
```

```python
import functools
import math

import jax
import jax.numpy as jnp
from jax import lax
from jax.experimental import pallas as pl
from jax.experimental.pallas import tpu as pltpu

F32 = jnp.float32
BF16 = jnp.bfloat16

D_MODEL = 2048
GRID_W = 64
CONV_CH = 1024
CONV_K = 31
FOURIER_CH = 1024
FOURIER_GROUPS = 4
FOURIER_GC = FOURIER_CH // FOURIER_GROUPS
MLA_NOPE = 128
MLA_ROPE = 64
MLA_V = 128
MLA_HEADS = 8
MLA_PAD = 256
Q_LORA = 512
KV_LORA = 512
SWA_HD = 64
SWA_HEADS = 16
SWA_KV_HEADS = 2
SWA_GROUP = SWA_HEADS // SWA_KV_HEADS
WINDOW = 128
Q_BLOCK = 128
ROPE_BASE = 10000.0
NORM_EPS = 1e-6
LN_EPS = 1e-5
LOG2E = math.log2(math.e)
MLA_QSCALE = (MLA_NOPE + MLA_ROPE) ** -0.5 * LOG2E
SWA_QSCALE = SWA_HD ** -0.5 * LOG2E
LANE = 128
SUBLANE = 8
HALO = 16
MOD_ROWS = 8
VMEM_LIMIT = 56 * 1024 * 1024

TM = 512
TM_FFN = 512
TH_FFN = 512
TQ_MLA = 256
MLA_HEADS_PER_STEP = 2
TT_CONV = 256
TK_DFT = 256
NEG_BIG = -1e30


def _cparams(*sem):
    return pltpu.CompilerParams(dimension_semantics=sem, vmem_limit_bytes=VMEM_LIMIT)


def _resident(shape):
    nd = len(shape)
    return pl.BlockSpec(shape, lambda *_: (0,) * nd, pipeline_mode=pl.Buffered(1))


def _layer_resident(shape, layer, block=0):
    return pl.BlockSpec((None,) + tuple(shape), lambda *_: (layer, block, 0), pipeline_mode=pl.Buffered(1))


def _rms_scale(x):
    return lax.rsqrt(jnp.mean(x * x, axis=-1, keepdims=True) + NORM_EPS)


def _rms(x, g):
    return x * _rms_scale(x) * g


def _norm_mod(x, g_ref, sh_ref, sc_ref):
    gain = g_ref[...] * (1.0 + sc_ref[...])
    return (x * _rms_scale(x) * gain + sh_ref[...]).astype(BF16)


def _gated_residual(h, y, g_ref, gate_ref):
    return h + y * _rms_scale(y) * (gate_ref[...] * g_ref[...])


def _h_operands(h, tm, n_lat, tile_of):
    if isinstance(h, tuple):
        lat_tiles = n_lat // tm
        specs = [pl.BlockSpec((tm, D_MODEL), lambda i: (jnp.minimum(tile_of(i), lat_tiles - 1), 0)),
                 pl.BlockSpec((tm, D_MODEL), lambda i: (jnp.maximum(tile_of(i) - lat_tiles, 0), 0))]
        return specs, list(h), lat_tiles
    specs = [pl.BlockSpec((tm, D_MODEL), lambda i: (tile_of(i), 0)), _resident((tm, D_MODEL))]
    return specs, [h, h], None


def _tile_h(hl_ref, hc_ref, ctx_start, tile):
    if ctx_start is None:
        return hl_ref[...]
    return jnp.where(tile >= ctx_start, hc_ref[...], hl_ref[...])


def _mod_spec(layer, chunk, mod_row_of_tile):
    def imap(i, *_):
        return ((layer * MOD_ROWS + mod_row_of_tile(i)) * 6 + chunk, 0, 0)
    return pl.BlockSpec((None, 1, D_MODEL), imap)


def _ada_kernel(c_ref, w_ref, b_ref, o_ref):
    c = c_ref[...]
    a = (c * jax.nn.sigmoid(c)).astype(BF16)
    o_ref[...] = jnp.dot(a, w_ref[...].astype(BF16), preferred_element_type=F32) + b_ref[...]


def _ada_mods(cvec, w_ada, b_ada):
    depth, d, n6 = w_ada.shape
    tn = 1024
    return pl.pallas_call(
        _ada_kernel,
        grid=(depth, n6 // tn),
        in_specs=[pl.BlockSpec((MOD_ROWS, d), lambda l, j: (0, 0)),
                  pl.BlockSpec((None, d, tn), lambda l, j: (l, 0, j)),
                  pl.BlockSpec((None, 1, tn), lambda l, j: (l, 0, j))],
        out_specs=pl.BlockSpec((None, MOD_ROWS, tn), lambda l, j: (l, 0, j)),
        out_shape=jax.ShapeDtypeStruct((depth, MOD_ROWS, n6), F32),
        compiler_params=_cparams("parallel", "parallel"),
        name="ada_mods",
    )(cvec, w_ada, b_ada.reshape(depth, 1, n6))


def _nm_matmul_kernel(hl_ref, hc_ref, g_ref, sh_ref, sc_ref, w_ref, o_ref, *, ctx_start):
    u = _norm_mod(_tile_h(hl_ref, hc_ref, ctx_start, pl.program_id(0)), g_ref, sh_ref, sc_ref)
    o_ref[...] = jnp.dot(u, w_ref[...], preferred_element_type=F32).astype(o_ref.dtype)


def _nm_matmul_dft_kernel(hl_ref, hc_ref, g_ref, sh_ref, sc_ref, w_ref, cs_ref, o_ref, zc_ref, zs_ref,
                          *, ctx_start):
    u = _norm_mod(_tile_h(hl_ref, hc_ref, ctx_start, pl.program_id(0)), g_ref, sh_ref, sc_ref)
    y = jnp.dot(u, w_ref[...], preferred_element_type=F32)
    n_conv = o_ref.shape[1]
    o_ref[...] = y[:, :n_conv].astype(o_ref.dtype)
    for grp in range(FOURIER_GROUPS):
        cols = slice(grp * FOURIER_GC, (grp + 1) * FOURIER_GC)
        f = y[:, n_conv + grp * FOURIER_GC:n_conv + (grp + 1) * FOURIER_GC].astype(BF16)
        z = jnp.dot(f, cs_ref[...], preferred_element_type=F32)
        zc_ref[:, cols] = z[:, :FOURIER_GC].astype(zc_ref.dtype)
        zs_ref[:, cols] = z[:, FOURIER_GC:].astype(zs_ref.dtype)


def _nm_matmul(h, g, mods, layer, w_stack, w_layer, n_rows, n_lat, mod_row, cs_c=None):
    nout = w_stack.shape[2]

    def rows(w):
        return pl.BlockSpec((TM, w), lambda i: (i, 0))

    def out(w):
        return jax.ShapeDtypeStruct((n_rows, w), BF16)

    h_specs, h_args, ctx_start = _h_operands(h, TM, n_lat, lambda i: i)
    in_specs = h_specs + [_resident((1, D_MODEL)),
                          _mod_spec(layer, 0, mod_row), _mod_spec(layer, 1, mod_row),
                          _layer_resident((D_MODEL, nout), w_layer)]
    args = h_args + [g, mods, mods, w_stack]
    if cs_c is None:
        kern, out_specs, out_shape = _nm_matmul_kernel, rows(nout), out(nout)
    else:
        n_conv = nout - FOURIER_CH
        kern = _nm_matmul_dft_kernel
        in_specs.append(_resident(cs_c.shape))
        args.append(cs_c)
        out_specs = [rows(n_conv), rows(FOURIER_CH), rows(FOURIER_CH)]
        out_shape = [out(n_conv), out(FOURIER_CH), out(FOURIER_CH)]
    return pl.pallas_call(
        functools.partial(kern, ctx_start=ctx_start),
        grid=(n_rows // TM,),
        in_specs=in_specs,
        out_specs=out_specs,
        out_shape=out_shape,
        compiler_params=_cparams("parallel"),
        name="nm_matmul",
    )(*args)


def _out_proj_kernel(y1_ref, y2_ref, w1_ref, w2_ref, hl_ref, hc_ref, g_ref, gate_ref, o_ref, *, ctx_start):
    y = (jnp.dot(y1_ref[...], w1_ref[...], preferred_element_type=F32)
         + jnp.dot(y2_ref[...], w2_ref[...], preferred_element_type=F32))
    o_ref[...] = _gated_residual(_tile_h(hl_ref, hc_ref, ctx_start, pl.program_id(0)), y, g_ref, gate_ref)


def _out_proj(y1, y2, w_stack, w_layer, h, g, mods, layer, n_rows, n_lat, mod_row):
    half = y1.shape[1]
    h_specs, h_args, ctx_start = _h_operands(h, TM, n_lat, lambda i: i)
    return pl.pallas_call(
        functools.partial(_out_proj_kernel, ctx_start=ctx_start),
        grid=(n_rows // TM,),
        in_specs=[pl.BlockSpec((TM, half), lambda i: (i, 0)),
                  pl.BlockSpec((TM, half), lambda i: (i, 0)),
                  _layer_resident((half, D_MODEL), w_layer, 0),
                  _layer_resident((half, D_MODEL), w_layer, 1)] + h_specs + [
                  _resident((1, D_MODEL)),
                  _mod_spec(layer, 2, mod_row)],
        out_specs=pl.BlockSpec((TM, D_MODEL), lambda i: (i, 0)),
        out_shape=jax.ShapeDtypeStruct((n_rows, D_MODEL), F32),
        compiler_params=_cparams("parallel"),
        name="out_proj",
    )(y1, y2, w_stack, w_stack, *h_args, g, mods)


def _ffn_kernel(h_ref, g2_ref, sh_ref, sc_ref, wg0_ref, wu0_ref, wd0_ref, wg1_ref, wu1_ref, wd1_ref,
                g3_ref, gate_ref, o_ref, u_ref, *, n_chunks):
    j = pl.program_id(1)

    @pl.when(j == 0)
    def _():
        u_ref[...] = _norm_mod(h_ref[...], g2_ref, sh_ref, sc_ref)
        o_ref[...] = jnp.zeros_like(o_ref)

    def contribution(wg_ref, wu_ref, wd_ref):
        u = u_ref[...]
        a = jnp.dot(u, wg_ref[...], preferred_element_type=F32)
        b = jnp.dot(u, wu_ref[...], preferred_element_type=F32)
        hid = (a * jax.nn.sigmoid(a) * b).astype(BF16)
        return jnp.dot(hid, wd_ref[...], preferred_element_type=F32)

    @pl.when(j < n_chunks // 2)
    def _():
        o_ref[...] += contribution(wg0_ref, wu0_ref, wd0_ref) + contribution(wg1_ref, wu1_ref, wd1_ref)

    if n_chunks % 2:
        @pl.when(j == n_chunks // 2)
        def _():
            o_ref[...] += contribution(wg0_ref, wu0_ref, wd0_ref)

    @pl.when(j == pl.num_programs(1) - 1)
    def _():
        o_ref[...] = _gated_residual(h_ref[...], o_ref[...], g3_ref, gate_ref)


def _ffn(h, g2, g3, mods, layer, wg, wu, wd, n_rows, mod_row):
    hidden = wg.shape[2]
    th = TH_FFN
    tm = TM_FFN
    n_chunks = hidden // th

    def chunk_specs(which):
        def chunk(j):
            return jnp.minimum(2 * j + which, n_chunks - 1)
        return [pl.BlockSpec((None, D_MODEL, th), lambda i, j: (layer, 0, chunk(j))),
                pl.BlockSpec((None, D_MODEL, th), lambda i, j: (layer, 0, chunk(j))),
                pl.BlockSpec((None, th, D_MODEL), lambda i, j: (layer, chunk(j), 0))]

    return pl.pallas_call(
        functools.partial(_ffn_kernel, n_chunks=n_chunks),
        grid=(n_rows // tm, (n_chunks + 1) // 2),
        in_specs=[pl.BlockSpec((tm, D_MODEL), lambda i, j: (i, 0)),
                  _resident((1, D_MODEL)),
                  _mod_spec(layer, 3, mod_row), _mod_spec(layer, 4, mod_row)]
                 + chunk_specs(0) + chunk_specs(1)
                 + [_resident((1, D_MODEL)), _mod_spec(layer, 5, mod_row)],
        out_specs=pl.BlockSpec((tm, D_MODEL), lambda i, j: (i, 0)),
        out_shape=jax.ShapeDtypeStruct((n_rows, D_MODEL), F32),
        scratch_shapes=[pltpu.VMEM((tm, D_MODEL), BF16)],
        compiler_params=_cparams("parallel", "arbitrary"),
        name="ffn",
    )(h, g2, mods, mods, wg, wu, wd, wg, wu, wd, g3, mods)


def _conv_kernel(v_ref, gt_ref, vp_ref, gp_ref, vn_ref, gn_ref, w_ref, cb_ref, lg_ref, lb_ref, o_ref,
                 buf_ref, sh_ref, acc_ref, *, lat_tiles, tiles_per_seq):
    i = pl.program_id(0)
    tt = v_ref.shape[0]
    pos = i % tiles_per_seq
    is_lat = i < lat_tiles
    has_prev = jnp.logical_and(is_lat, pos != 0)
    has_next = jnp.logical_and(is_lat, pos != tiles_per_seq - 1)

    def glu(v, g):
        return v[...].astype(F32) * jax.nn.sigmoid(g[...].astype(F32))

    buf_ref[0:HALO, :] = jnp.where(has_prev, glu(vp_ref, gp_ref), 0.0)
    buf_ref[HALO:HALO + tt, :] = glu(v_ref, gt_ref)
    buf_ref[HALO + tt:, :] = jnp.where(has_next, glu(vn_ref, gn_ref), 0.0)

    ext = sh_ref.shape[1]
    off = HALO - CONV_K // 2
    for c in range(CONV_CH // LANE):
        lanes = slice(c * LANE, (c + 1) * LANE)
        col = buf_ref[:, lanes]
        for s in range(SUBLANE):
            sh_ref[s, :, lanes] = col[s:s + ext, :]
    for c in range(CONV_CH // LANE):
        lanes = slice(c * LANE, (c + 1) * LANE)
        acc = jnp.broadcast_to(cb_ref[:, lanes], (tt, LANE))
        for k in range(CONV_K):
            s = (k + off) % SUBLANE
            base = (k + off) - s
            acc = acc + w_ref[k:k + 1, lanes] * sh_ref[s, base:base + tt, lanes]
        acc_ref[:, lanes] = acc

    rc = 32
    for r in range(tt // rc):
        x = acc_ref[r * rc:(r + 1) * rc, :]
        mu = jnp.mean(x, axis=-1, keepdims=True)
        xc = x - mu
        var = jnp.mean(xc * xc, axis=-1, keepdims=True)
        y = xc * lax.rsqrt(var + LN_EPS) * lg_ref[...] + lb_ref[...]
        o_ref[r * rc:(r + 1) * rc, :] = (y * jax.nn.sigmoid(y)).astype(o_ref.dtype)


def _conv_branch(p, conv_w, conv_b, ln_g, ln_b, n_lat, seq_lat, seq_ctx, n_rows):
    tt = TT_CONV
    assert seq_ctx == tt and seq_lat % tt == 0
    hb = tt // HALO
    last_hb = n_rows // HALO - 1
    kern = functools.partial(_conv_kernel, lat_tiles=n_lat // tt, tiles_per_seq=seq_lat // tt)

    def prev_map(col):
        return lambda i: (jnp.maximum(i * hb - 1, 0), col)

    def next_map(col):
        return lambda i: (jnp.minimum((i + 1) * hb, last_hb), col)

    return pl.pallas_call(
        kern,
        grid=(n_rows // tt,),
        in_specs=[pl.BlockSpec((tt, CONV_CH), lambda i: (i, 0)),
                  pl.BlockSpec((tt, CONV_CH), lambda i: (i, 1)),
                  pl.BlockSpec((HALO, CONV_CH), prev_map(0)),
                  pl.BlockSpec((HALO, CONV_CH), prev_map(1)),
                  pl.BlockSpec((HALO, CONV_CH), next_map(0)),
                  pl.BlockSpec((HALO, CONV_CH), next_map(1)),
                  _resident((CONV_K, CONV_CH)),
                  _resident((1, CONV_CH)), _resident((1, CONV_CH)), _resident((1, CONV_CH))],
        out_specs=pl.BlockSpec((tt, CONV_CH), lambda i: (i, 0)),
        out_shape=jax.ShapeDtypeStruct((n_rows, CONV_CH), BF16),
        scratch_shapes=[pltpu.VMEM((tt + 2 * HALO, CONV_CH), F32),
                        pltpu.VMEM((SUBLANE, tt + 2 * HALO - SUBLANE, CONV_CH), F32),
                        pltpu.VMEM((tt, CONV_CH), F32)],
        compiler_params=_cparams("parallel"),
        name="conv_branch",
    )(p, p, p, p, p, p, conv_w, conv_b, ln_g, ln_b)


def _dft_pos_kernel(ct_ref, st_ref, zc_ref, zs_ref, ctx_ct_ref, ctx_st_ref, zcx_ref, zsx_ref, o_ref, *, lat_steps):
    def run(c_ref, s_ref, a_ref, b_ref):
        y = (jnp.dot(c_ref[...], a_ref[...], preferred_element_type=F32)
             + jnp.dot(s_ref[...], b_ref[...], preferred_element_type=F32))
        o_ref[...] = y.astype(o_ref.dtype)

    is_ctx = pl.program_id(1) >= lat_steps
    pl.when(is_ctx)(functools.partial(run, ctx_ct_ref, ctx_st_ref, zcx_ref, zsx_ref))
    pl.when(jnp.logical_not(is_ctx))(functools.partial(run, ct_ref, st_ref, zc_ref, zs_ref))


def _dft_positions(ct, nst, ct_ctx, nst_ctx, zc, zs, nbatch, seq_lat, seq_ctx):
    tk = TK_DFT
    assert seq_ctx == tk and seq_lat % tk == 0
    lat_steps = seq_lat // tk
    ctx_blk0 = nbatch * lat_steps

    def tab_map(b, k):
        return (jnp.minimum(k, lat_steps - 1), 0)

    def out_map(b, k):
        return (jnp.where(k < lat_steps, b * lat_steps + k, ctx_blk0 + b), 0)

    lat_tab = pl.BlockSpec((tk, seq_lat), tab_map)
    lat_z = pl.BlockSpec((seq_lat, FOURIER_CH), lambda b, k: (b, 0))
    ctx_z = pl.BlockSpec((seq_ctx, FOURIER_CH), lambda b, k: (ctx_blk0 + b, 0))
    return pl.pallas_call(
        functools.partial(_dft_pos_kernel, lat_steps=lat_steps),
        grid=(nbatch, lat_steps + 1),
        in_specs=[lat_tab, lat_tab, lat_z, lat_z,
                  _resident((seq_ctx, seq_ctx)), _resident((seq_ctx, seq_ctx)), ctx_z, ctx_z],
        out_specs=pl.BlockSpec((tk, FOURIER_CH), out_map),
        out_shape=jax.ShapeDtypeStruct((nbatch * (seq_lat + seq_ctx), FOURIER_CH), BF16),
        compiler_params=_cparams("parallel", "arbitrary"),
        name="dft_positions",
    )(ct, nst, zc, zs, ct_ctx, nst_ctx, zc, zs)


def _dft_tables(n, scale):
    k = jnp.arange(n, dtype=jnp.int32)
    m = (k[:, None] * k[None, :]) % n
    ang = m.astype(F32) * (2.0 * jnp.pi / n)
    return (jnp.cos(ang) * scale).astype(BF16), (-jnp.sin(ang) * scale).astype(BF16)


def _dft_tables_big(n, scale):
    r = int(round(n ** 0.5))
    assert r * r == n
    t = jnp.arange(n, dtype=jnp.int32)[None, :]
    kk = jnp.arange(r, dtype=jnp.int32)[:, None]
    a_hi = ((kk * r * t) % n).astype(F32) * (2.0 * jnp.pi / n)
    a_lo = ((kk * t) % n).astype(F32) * (2.0 * jnp.pi / n)
    ch, sh, cl, sl = jnp.cos(a_hi)[:, None, :], jnp.sin(a_hi)[:, None, :], jnp.cos(a_lo)[None], jnp.sin(a_lo)[None]
    c = (ch * cl - sh * sl) * scale
    s = (sh * cl + ch * sl) * scale
    return c.reshape(n, n).astype(BF16), (-s).reshape(n, n).astype(BF16)


def _rope(x, cos, sin_signed, first_half):
    swapped = jnp.where(first_half, pltpu.roll(x, LANE - 16, 1), pltpu.roll(x, 16, 1))
    return x * cos + swapped * sin_signed


def _odd_proj_kernel(p_ref, cos_ref, sin_ref, gq_ref, gkv_ref, wuq_ref, wuk_ref, wuv_ref,
                     q_ref, k_ref, v_ref, qs_ref, ks_ref, vs_ref):
    tm = p_ref.shape[0]
    cos = cos_ref[...]
    sin = sin_ref[...]
    lane = lax.broadcasted_iota(jnp.int32, (tm, LANE), 1)
    first_half = (lane % 32) < 16
    rope = functools.partial(_rope, cos=cos, sin_signed=sin, first_half=first_half)

    o_kv = Q_LORA
    o_kr = o_kv + KV_LORA
    o_qs = o_kr + LANE
    o_ks = o_qs + SWA_HEADS * SWA_HD
    o_vs = o_ks + SWA_KV_HEADS * SWA_HD

    nq = _rms(p_ref[:, 0:Q_LORA].astype(F32), gq_ref[...]).astype(BF16)
    nkv = _rms(p_ref[:, o_kv:o_kr].astype(F32), gkv_ref[...]).astype(BF16)
    q = jnp.dot(nq, wuq_ref[...], preferred_element_type=F32) * MLA_QSCALE
    kn = jnp.dot(nkv, wuk_ref[...], preferred_element_type=F32)
    v = jnp.dot(nkv, wuv_ref[...], preferred_element_type=F32)
    kr = rope(p_ref[:, o_kr:o_qs].astype(F32)).astype(k_ref.dtype)
    ones = jnp.ones((tm, LANE), v_ref.dtype)
    for h in range(MLA_HEADS):
        c0 = h * MLA_PAD
        q_ref[:, c0:c0 + LANE] = q[:, c0:c0 + LANE].astype(q_ref.dtype)
        q_ref[:, c0 + LANE:c0 + 2 * LANE] = rope(q[:, c0 + LANE:c0 + 2 * LANE]).astype(q_ref.dtype)
        k_ref[:, c0:c0 + LANE] = kn[:, h * MLA_NOPE:(h + 1) * MLA_NOPE].astype(k_ref.dtype)
        k_ref[:, c0 + LANE:c0 + 2 * LANE] = kr
        v_ref[:, c0:c0 + LANE] = v[:, h * MLA_V:(h + 1) * MLA_V].astype(v_ref.dtype)
        v_ref[:, c0 + LANE:c0 + 2 * LANE] = ones
    for j in range(SWA_HEADS * SWA_HD // LANE):
        x = p_ref[:, o_qs + j * LANE:o_qs + (j + 1) * LANE].astype(F32) * SWA_QSCALE
        qs_ref[:, j * LANE:(j + 1) * LANE] = rope(x).astype(qs_ref.dtype)
    ks_ref[...] = rope(p_ref[:, o_ks:o_vs].astype(F32)).astype(ks_ref.dtype)
    vs_ref[...] = p_ref[:, o_vs:o_vs + LANE]


def _odd_proj(p, cos_t, sin_t, gq, gkv, wuq, wuk, wuv, layer, n_lat, seq_lat, n_rows):
    tm = TM
    lat_tiles = n_lat // tm
    per_seq = seq_lat // tm

    def tab_map(i):
        return (jnp.where(i < lat_tiles, i % per_seq, per_seq), 0)

    def rows(w):
        return pl.BlockSpec((tm, w), lambda i: (i, 0))

    def out(w):
        return jax.ShapeDtypeStruct((n_rows, w), BF16)

    pad_w = MLA_HEADS * MLA_PAD
    return pl.pallas_call(
        _odd_proj_kernel,
        grid=(n_rows // tm,),
        in_specs=[rows(p.shape[1]),
                  pl.BlockSpec((tm, LANE), tab_map), pl.BlockSpec((tm, LANE), tab_map),
                  _resident((1, Q_LORA)), _resident((1, KV_LORA)),
                  _layer_resident(wuq.shape[1:], layer), _layer_resident(wuk.shape[1:], layer),
                  _layer_resident(wuv.shape[1:], layer)],
        out_specs=[rows(pad_w), rows(pad_w), rows(pad_w), rows(SWA_HEADS * SWA_HD), rows(LANE), rows(LANE)],
        out_shape=[out(pad_w), out(pad_w), out(pad_w), out(SWA_HEADS * SWA_HD), out(LANE), out(LANE)],
        compiler_params=_cparams("parallel"),
        name="odd_proj",
    )(p, cos_t, sin_t, gq, gkv, wuq, wuk, wuv)


def _rope_tables(seq, pad_rows):
    quarter = MLA_ROPE // 4
    inv = ROPE_BASE ** (-jnp.arange(quarter, dtype=F32) / quarter)
    t = jnp.arange(seq, dtype=jnp.int32)
    row = (t // GRID_W).astype(F32)[:, None] * inv
    col = (t % GRID_W).astype(F32)[:, None] * inv
    ang = jnp.concatenate([row, row, col, col], axis=-1)
    sign = jnp.concatenate([-jnp.ones((quarter,), F32), jnp.ones((quarter,), F32)] * 2)
    cos = jnp.cos(ang)
    sin = jnp.sin(ang) * sign
    cos = jnp.concatenate([jnp.tile(cos, (1, 2)), jnp.ones((pad_rows, LANE), F32)], axis=0)
    sin = jnp.concatenate([jnp.tile(sin, (1, 2)), jnp.zeros((pad_rows, LANE), F32)], axis=0)
    return cos, sin


def _qk(q, k):
    return lax.dot_general(q, k, (((1,), (1,)), ((), ())), preferred_element_type=F32)


def _pipeline_cases(i, lat_tiles, ctx_tiles, stage_a, stage_b):
    n = lat_tiles + ctx_tiles
    groups = {}
    for step in range(n + 1):
        has_a, has_b = step < n, step >= 1
        key = (has_a, has_a and step >= lat_tiles, has_b, has_b and step - 1 >= lat_tiles, step % 2)
        groups.setdefault(key, []).append(step)
    for (has_a, a_ctx, has_b, b_ctx, slot), steps in groups.items():
        def body(has_a=has_a, a_ctx=a_ctx, has_b=has_b, b_ctx=b_ctx, slot=slot):
            if has_a:
                stage_a(a_ctx, slot)
            if has_b:
                stage_b(b_ctx, 1 - slot)
        pl.when((i >= steps[0]) & (i <= steps[-1]) & (i % 2 == slot))(body)


def _mla_kernel(q_ref, kl_ref, kc_ref, vl_ref, vc_ref, o_ref, *scratch, heads, lat_tiles, ctx_tiles):
    i = pl.program_id(2)
    n_lat_keys = kl_ref.shape[0]

    def buffers(slot, head):
        base = 2 * (slot * heads + head)
        return scratch[base], scratch[base + 1]

    def stage_a(tile_is_ctx, slot):
        for head in range(heads):
            s_ref, m_ref = buffers(slot, head)
            cols = slice(head * MLA_PAD, (head + 1) * MLA_PAD)
            q = q_ref[:, cols]
            sc = _qk(q, kc_ref[:, cols])
            m = jnp.max(sc, axis=-1, keepdims=True)
            if not tile_is_ctx:
                sl = _qk(q, kl_ref[:, cols])
                m = jnp.maximum(m, jnp.max(sl, axis=-1, keepdims=True))
                s_ref[:, :n_lat_keys] = sl
            s_ref[:, n_lat_keys:] = sc
            m_ref[...] = jnp.broadcast_to(m, m_ref.shape)

    def stage_b(tile_is_ctx, slot):
        for head in range(heads):
            s_ref, m_ref = buffers(slot, head)
            cols = slice(head * MLA_PAD, (head + 1) * MLA_PAD)
            m = m_ref[:, 0:1]
            o = jnp.dot(jnp.exp2(s_ref[:, n_lat_keys:] - m).astype(BF16), vc_ref[:, cols],
                        preferred_element_type=F32)
            if not tile_is_ctx:
                o = o + jnp.dot(jnp.exp2(s_ref[:, :n_lat_keys] - m).astype(BF16), vl_ref[:, cols],
                                preferred_element_type=F32)
            o_ref[:, head * MLA_V:(head + 1) * MLA_V] = (o[:, :MLA_V] / o[:, MLA_V:]).astype(o_ref.dtype)

    _pipeline_cases(i, lat_tiles, ctx_tiles, stage_a, stage_b)


def _mla_attention(q, k, v, nbatch, seq_lat, seq_ctx, ctx_queries):
    tq = TQ_MLA
    n_lat = nbatch * seq_lat
    lat_tiles = seq_lat // tq
    ctx_tiles = seq_ctx // tq if ctx_queries else 0
    n_tiles = lat_tiles + ctx_tiles
    out_rows = n_lat + (nbatch * seq_ctx if ctx_queries else 0)
    ctx_blk0 = n_lat // seq_ctx

    def tile_row(b, t):
        return jnp.where(t < lat_tiles, b * lat_tiles + t, n_lat // tq + b * ctx_tiles + (t - lat_tiles))

    hp = MLA_HEADS_PER_STEP
    kern = functools.partial(_mla_kernel, heads=hp, lat_tiles=lat_tiles, ctx_tiles=ctx_tiles)
    n_keys = seq_lat + seq_ctx
    width = hp * MLA_PAD
    return pl.pallas_call(
        kern,
        grid=(nbatch, MLA_HEADS // hp, n_tiles + 1),
        in_specs=[pl.BlockSpec((tq, width), lambda b, h, i: (tile_row(b, jnp.minimum(i, n_tiles - 1)), h)),
                  pl.BlockSpec((seq_lat, width), lambda b, h, i: (b, h)),
                  pl.BlockSpec((seq_ctx, width), lambda b, h, i: (ctx_blk0 + b, h)),
                  pl.BlockSpec((seq_lat, width), lambda b, h, i: (b, h)),
                  pl.BlockSpec((seq_ctx, width), lambda b, h, i: (ctx_blk0 + b, h))],
        out_specs=pl.BlockSpec((tq, hp * MLA_V), lambda b, h, i: (tile_row(b, jnp.maximum(i - 1, 0)), h)),
        out_shape=jax.ShapeDtypeStruct((out_rows, MLA_HEADS * MLA_V), BF16),
        scratch_shapes=[pltpu.VMEM((tq, n_keys), F32), pltpu.VMEM((tq, LANE), F32)] * (2 * hp),
        compiler_params=_cparams("parallel", "parallel", "arbitrary"),
        name="mla_attention",
    )(q, k, k, v, v)


def _swa_kernel(sink_ref, q_ref, kp_ref, kc_ref, kn_ref, kx_ref, vp_ref, vc_ref, vn_ref, vx_ref, o_ref,
                s0_ref, m0_ref, s1_ref, m1_ref, *, lat_tiles, ctx_tiles, seq_lat):
    i = pl.program_id(1)
    nx = kx_ref.shape[0]
    lo = lax.broadcasted_iota(jnp.int32, (Q_BLOCK, LANE), 1) < LANE // 2
    slots = ((s0_ref, m0_ref), (s1_ref, m1_ref))

    def stage_a(tile_is_ctx, slot):
        s_ref, m_ref = slots[slot]
        zero = jnp.zeros((Q_BLOCK, LANE), q_ref.dtype)
        chunks = [q_ref[:, j * LANE:(j + 1) * LANE] for j in range(SWA_GROUP)]
        qstack = jnp.concatenate([jnp.where(lo, ch, zero) for ch in chunks]
                                 + [jnp.where(lo, zero, ch) for ch in chunks], axis=0)
        if tile_is_ctx:
            s = _qk(qstack, kx_ref[...])
            bias = None
        else:
            k_all = jnp.concatenate([kx_ref[...], kp_ref[...], kc_ref[...], kn_ref[...]], axis=0)
            nk = k_all.shape[0]
            qpos = i * Q_BLOCK + lax.broadcasted_iota(jnp.int32, (Q_BLOCK, nk), 0)
            col = lax.broadcasted_iota(jnp.int32, (Q_BLOCK, nk), 1)
            kpos = (i - 1) * Q_BLOCK + (col - nx)
            in_window = (kpos >= 0) & (kpos < seq_lat) & (jnp.abs(qpos - kpos) <= WINDOW)
            bias = jnp.where((col < nx) | in_window, 0.0, NEG_BIG)
            s = _qk(qstack, k_all)
        nk = s.shape[1]
        for r in range(SWA_HEADS):
            rows = slice(r * Q_BLOCK, (r + 1) * Q_BLOCK)
            blk = s[rows] if bias is None else s[rows] + bias
            s_ref[rows, :nk] = blk
            m = jnp.broadcast_to(jnp.max(blk, axis=-1, keepdims=True), (Q_BLOCK, LANE))
            m_ref[rows, :] = jnp.maximum(m, sink_ref[rows, :])

    def stage_b(tile_is_ctx, slot):
        s_ref, m_ref = slots[slot]
        m = m_ref[...]
        if tile_is_ctx:
            nk = nx
            v_all = vx_ref[...]
        else:
            nk = s_ref.shape[1]
            v_all = jnp.concatenate([vx_ref[...], vp_ref[...], vc_ref[...], vn_ref[...]], axis=0)
        v_ext = jnp.concatenate([v_all, jnp.ones_like(v_all)], axis=1)
        p = jnp.concatenate([jnp.exp2(s_ref[:, c * LANE:(c + 1) * LANE] - m).astype(BF16)
                             for c in range(nk // LANE)], axis=1)
        o = jnp.dot(p, v_ext, preferred_element_type=F32)
        o = o[:, :LANE] / (o[:, LANE:] + jnp.exp2(sink_ref[...] - m))
        for j in range(SWA_GROUP):
            pair = jnp.where(lo, o[j * Q_BLOCK:(j + 1) * Q_BLOCK],
                             o[(SWA_GROUP + j) * Q_BLOCK:(SWA_GROUP + j + 1) * Q_BLOCK])
            o_ref[:, j * LANE:(j + 1) * LANE] = pair.astype(o_ref.dtype)

    _pipeline_cases(i, lat_tiles, ctx_tiles, stage_a, stage_b)


def _swa_attention(sink_rows, qs, ks, vs, nbatch, seq_lat, seq_ctx, ctx_queries):
    qb = Q_BLOCK
    n_lat = nbatch * seq_lat
    lat_tiles = seq_lat // qb
    ctx_tiles = seq_ctx // qb if ctx_queries else 0
    n_tiles = lat_tiles + ctx_tiles
    out_rows = n_lat + (nbatch * seq_ctx if ctx_queries else 0)
    ctx_blk0 = n_lat // seq_ctx
    n_keys = seq_ctx + 3 * qb

    def tile_row(b, t):
        return jnp.where(t < lat_tiles, b * lat_tiles + t, n_lat // qb + b * ctx_tiles + (t - lat_tiles))

    def window(offset):
        def imap(b, i):
            return (b * lat_tiles + jnp.clip(i + offset, 0, lat_tiles - 1), 0)
        return pl.BlockSpec((qb, LANE), imap)

    xblk = pl.BlockSpec((seq_ctx, LANE), lambda b, i: (ctx_blk0 + b, 0))
    kern = functools.partial(_swa_kernel, lat_tiles=lat_tiles, ctx_tiles=ctx_tiles, seq_lat=seq_lat)
    width = SWA_HEADS * SWA_HD
    return pl.pallas_call(
        kern,
        grid=(nbatch, n_tiles + 1),
        in_specs=[_resident(sink_rows.shape),
                  pl.BlockSpec((qb, width), lambda b, i: (tile_row(b, jnp.minimum(i, n_tiles - 1)), 0)),
                  window(-1), window(0), window(1), xblk,
                  window(-2), window(-1), window(0), xblk],
        out_specs=pl.BlockSpec((qb, width), lambda b, i: (tile_row(b, jnp.maximum(i - 1, 0)), 0)),
        out_shape=jax.ShapeDtypeStruct((out_rows, width), BF16),
        scratch_shapes=[pltpu.VMEM((SWA_HEADS * qb, n_keys), F32), pltpu.VMEM((SWA_HEADS * qb, LANE), F32),
                        pltpu.VMEM((SWA_HEADS * qb, n_keys), F32), pltpu.VMEM((SWA_HEADS * qb, LANE), F32)],
        compiler_params=_cparams("parallel", "arbitrary"),
        name="swa_attention",
    )(sink_rows, qs, ks, ks, ks, ks, vs, vs, vs, vs)


def _pair_heads(w, axis):
    shape = w.shape
    w = w.reshape(shape[:axis] + (SWA_KV_HEADS, SWA_GROUP, SWA_HD) + shape[axis + 1:])
    w = jnp.swapaxes(w, axis, axis + 1)
    return w.reshape(shape)


def _odd_in_layout(w):
    o2 = Q_LORA + KV_LORA
    o3 = o2 + MLA_ROPE
    o4 = o3 + SWA_HEADS * SWA_HD
    kr = jnp.pad(w[..., o2:o3], ((0, 0), (0, 0), (0, LANE - MLA_ROPE)))
    return jnp.concatenate([w[..., :o2], kr, _pair_heads(w[..., o3:o4], 2), w[..., o4:]], axis=-1).astype(BF16)


def _uq_layout(w):
    per = MLA_NOPE + MLA_ROPE
    w = w.reshape(w.shape[:2] + (MLA_HEADS, per))
    w = jnp.pad(w, ((0, 0), (0, 0), (0, 0), (0, MLA_PAD - per)))
    return w.reshape(w.shape[:2] + (MLA_HEADS * MLA_PAD,)).astype(BF16)


def _out_odd_layout(w):
    half = w.shape[1] // 2
    return jnp.concatenate([w[:, :half], _pair_heads(w[:, half:], 1)], axis=1).astype(BF16)


def kernel(x, c, ctx, c_ctx, w_ada, b_ada, norm_g, w_in_even, conv_w, conv_b, conv_ln_g, conv_ln_b,
           w_in_odd, q_norm_g, kv_norm_g, w_uq, w_uk, w_uv, sink, w_out, w_gate, w_up, w_down):
    nbatch, seq, d = x.shape
    seq_ctx = ctx.shape[1]
    depth = w_ada.shape[0]
    n_lat = nbatch * seq
    n_all = n_lat + nbatch * seq_ctx

    def mod_row_for(tm):
        return lambda i: jnp.where(i < n_lat // tm, i // (seq // tm), nbatch)

    mod_row = mod_row_for(TM)

    cvec = jnp.concatenate([c, c_ctx[None, :], jnp.zeros((MOD_ROWS - nbatch - 1, d), F32)], axis=0)
    mods = _ada_mods(cvec, w_ada, b_ada).reshape(depth * MOD_ROWS * 6, 1, d)

    h = (x.reshape(n_lat, d), ctx.reshape(nbatch * seq_ctx, d))

    w_even = w_in_even.astype(BF16)
    w_odd = _odd_in_layout(w_in_odd)
    wuq, wuk, wuv = _uq_layout(w_uq), w_uk.astype(BF16), w_uv.astype(BF16)
    wo_even = w_out[0::2].astype(BF16)
    wo_odd = _out_odd_layout(w_out[1::2])
    wg, wu, wd = w_gate.astype(BF16), w_up.astype(BF16), w_down.astype(BF16)
    sink_rows = jnp.repeat(sink * LOG2E, Q_BLOCK, axis=1)[:, :, None]
    sink_rows = jnp.broadcast_to(sink_rows, sink_rows.shape[:2] + (LANE,))

    ortho = 1.0 / float((seq * FOURIER_GC) ** 0.5)
    ortho_ctx = 1.0 / float((seq_ctx * FOURIER_GC) ** 0.5)
    ct_lat, nst_lat = _dft_tables_big(seq, ortho)
    ct_ctx, nst_ctx = _dft_tables(seq_ctx, ortho_ctx)
    cc, ncs = _dft_tables(FOURIER_GC, 1.0)
    cs_c = jnp.concatenate([cc, -ncs], axis=1)
    cos_t, sin_t = _rope_tables(seq, TM)

    for l in range(depth):
        last = l == depth - 1
        j = l // 2
        g = norm_g[l][:, None, :]
        rows_out = n_lat if last else n_all
        if l % 2 == 0:
            p, zc, zs = _nm_matmul(h, g[0], mods, l, w_even, j, n_all, n_lat, mod_row, cs_c)
            y1 = _conv_branch(p, conv_w[j], conv_b[j][None], conv_ln_g[j][None], conv_ln_b[j][None],
                              n_lat, seq, seq_ctx, n_all)
            y2 = _dft_positions(ct_lat, nst_lat, ct_ctx, nst_ctx, zc, zs, nbatch, seq, seq_ctx)
            wo = wo_even
        else:
            p = _nm_matmul(h, g[0], mods, l, w_odd, j, n_all, n_lat, mod_row)
            q, k, v, qs, ks, vs = _odd_proj(p, cos_t, sin_t, q_norm_g[j][None], kv_norm_g[j][None],
                                            wuq, wuk, wuv, j, n_lat, seq, n_all)
            y1 = _mla_attention(q, k, v, nbatch, seq, seq_ctx, not last)
            y2 = _swa_attention(sink_rows[j], qs, ks, vs, nbatch, seq, seq_ctx, not last)
            wo = wo_odd
        h = _out_proj(y1, y2, wo, j, h, g[1], mods, l, rows_out, n_lat, mod_row)
        h = _ffn(h, g[2], g[3], mods, l, wg, wu, wd, rows_out, mod_row_for(TM_FFN))
    return h.reshape(nbatch, seq, d)
```

```python
import functools
import math

import jax
import jax.numpy as jnp
from jax import lax
from jax.experimental import pallas as pl
from jax.experimental.pallas import tpu as pltpu

F32 = jnp.float32
BF16 = jnp.bfloat16

D_MODEL = 2048
GRID_W = 64
CONV_CH = 1024
CONV_K = 31
FOURIER_CH = 1024
FOURIER_GROUPS = 4
FOURIER_GC = FOURIER_CH // FOURIER_GROUPS
MLA_NOPE = 128
MLA_ROPE = 64
MLA_V = 128
MLA_HEADS = 8
MLA_PAD = 256
Q_LORA = 512
KV_LORA = 512
SWA_HD = 64
SWA_HEADS = 16
SWA_KV_HEADS = 2
SWA_GROUP = SWA_HEADS // SWA_KV_HEADS
WINDOW = 128
Q_BLOCK = 128
ROPE_BASE = 10000.0
NORM_EPS = 1e-6
LN_EPS = 1e-5
LOG2E = math.log2(math.e)
MLA_QSCALE = (MLA_NOPE + MLA_ROPE) ** -0.5 * LOG2E
SWA_QSCALE = SWA_HD ** -0.5 * LOG2E
LANE = 128
SUBLANE = 8
HALO = 16
MOD_ROWS = 8
VMEM_LIMIT = 56 * 1024 * 1024

TM = 512
TM_FFN = 512
TH_FFN = 512
TQ_MLA = 256
MLA_HEADS_PER_STEP = 2
TT_CONV = 256
TK_DFT = 256
NEG_BIG = -1e30


def _cparams(*sem):
    return pltpu.CompilerParams(dimension_semantics=sem, vmem_limit_bytes=VMEM_LIMIT)


def _resident(shape):
    nd = len(shape)
    return pl.BlockSpec(shape, lambda *_: (0,) * nd, pipeline_mode=pl.Buffered(1))


def _layer_resident(shape, layer, block=0):
    return pl.BlockSpec((None,) + tuple(shape), lambda *_: (layer, block, 0), pipeline_mode=pl.Buffered(1))


def _rms_scale(x):
    return lax.rsqrt(jnp.mean(x * x, axis=-1, keepdims=True) + NORM_EPS)


def _rms(x, g):
    return x * _rms_scale(x) * g


def _norm_mod(x, g_ref, sh_ref, sc_ref):
    gain = g_ref[...] * (1.0 + sc_ref[...])
    return (x * _rms_scale(x) * gain + sh_ref[...]).astype(BF16)


def _gated_residual(h, y, g_ref, gate_ref):
    return h + y * _rms_scale(y) * (gate_ref[...] * g_ref[...])


def _h_operands(h, tm, n_lat, tile_of):
    if isinstance(h, tuple):
        lat_tiles = n_lat // tm
        specs = [pl.BlockSpec((tm, D_MODEL), lambda i: (jnp.minimum(tile_of(i), lat_tiles - 1), 0)),
                 pl.BlockSpec((tm, D_MODEL), lambda i: (jnp.maximum(tile_of(i) - lat_tiles, 0), 0))]
        return specs, list(h), lat_tiles
    specs = [pl.BlockSpec((tm, D_MODEL), lambda i: (tile_of(i), 0)), _resident((tm, D_MODEL))]
    return specs, [h, h], None


def _tile_h(hl_ref, hc_ref, ctx_start, tile):
    if ctx_start is None:
        return hl_ref[...]
    return jnp.where(tile >= ctx_start, hc_ref[...], hl_ref[...])


def _mod_spec(layer, chunk, mod_row_of_tile):
    def imap(i, *_):
        return ((layer * MOD_ROWS + mod_row_of_tile(i)) * 6 + chunk, 0, 0)
    return pl.BlockSpec((None, 1, D_MODEL), imap)


def _ada_kernel(c_ref, w_ref, b_ref, o_ref):
    c = c_ref[...]
    a = (c * jax.nn.sigmoid(c)).astype(BF16)
    o_ref[...] = jnp.dot(a, w_ref[...].astype(BF16), preferred_element_type=F32) + b_ref[...]


def _ada_mods(cvec, w_ada, b_ada):
    depth, d, n6 = w_ada.shape
    tn = 1024
    return pl.pallas_call(
        _ada_kernel,
        grid=(depth, n6 // tn),
        in_specs=[pl.BlockSpec((MOD_ROWS, d), lambda l, j: (0, 0)),
                  pl.BlockSpec((None, d, tn), lambda l, j: (l, 0, j)),
                  pl.BlockSpec((None, 1, tn), lambda l, j: (l, 0, j))],
        out_specs=pl.BlockSpec((None, MOD_ROWS, tn), lambda l, j: (l, 0, j)),
        out_shape=jax.ShapeDtypeStruct((depth, MOD_ROWS, n6), F32),
        compiler_params=_cparams("parallel", "parallel"),
        name="ada_mods",
    )(cvec, w_ada, b_ada.reshape(depth, 1, n6))


def _nm_matmul_kernel(hl_ref, hc_ref, g_ref, sh_ref, sc_ref, w_ref, o_ref, *, ctx_start):
    u = _norm_mod(_tile_h(hl_ref, hc_ref, ctx_start, pl.program_id(0)), g_ref, sh_ref, sc_ref)
    o_ref[...] = jnp.dot(u, w_ref[...], preferred_element_type=F32).astype(o_ref.dtype)


def _nm_matmul_dft_kernel(hl_ref, hc_ref, g_ref, sh_ref, sc_ref, w_ref, cs_ref, o_ref, zc_ref, zs_ref,
                          *, ctx_start):
    u = _norm_mod(_tile_h(hl_ref, hc_ref, ctx_start, pl.program_id(0)), g_ref, sh_ref, sc_ref)
    y = jnp.dot(u, w_ref[...], preferred_element_type=F32)
    n_conv = o_ref.shape[1]
    o_ref[...] = y[:, :n_conv].astype(o_ref.dtype)
    for grp in range(FOURIER_GROUPS):
        cols = slice(grp * FOURIER_GC, (grp + 1) * FOURIER_GC)
        f = y[:, n_conv + grp * FOURIER_GC:n_conv + (grp + 1) * FOURIER_GC].astype(BF16)
        z = jnp.dot(f, cs_ref[...], preferred_element_type=F32)
        zc_ref[:, cols] = z[:, :FOURIER_GC].astype(zc_ref.dtype)
        zs_ref[:, cols] = z[:, FOURIER_GC:].astype(zs_ref.dtype)


def _nm_matmul(h, g, mods, layer, w_stack, w_layer, n_rows, n_lat, mod_row, cs_c=None):
    nout = w_stack.shape[2]

    def rows(w):
        return pl.BlockSpec((TM, w), lambda i: (i, 0))

    def out(w):
        return jax.ShapeDtypeStruct((n_rows, w), BF16)

    h_specs, h_args, ctx_start = _h_operands(h, TM, n_lat, lambda i: i)
    in_specs = h_specs + [_resident((1, D_MODEL)),
                          _mod_spec(layer, 0, mod_row), _mod_spec(layer, 1, mod_row),
                          _layer_resident((D_MODEL, nout), w_layer)]
    args = h_args + [g, mods, mods, w_stack]
    if cs_c is None:
        kern, out_specs, out_shape = _nm_matmul_kernel, rows(nout), out(nout)
    else:
        n_conv = nout - FOURIER_CH
        kern = _nm_matmul_dft_kernel
        in_specs.append(_resident(cs_c.shape))
        args.append(cs_c)
        out_specs = [rows(n_conv), rows(FOURIER_CH), rows(FOURIER_CH)]
        out_shape = [out(n_conv), out(FOURIER_CH), out(FOURIER_CH)]
    return pl.pallas_call(
        functools.partial(kern, ctx_start=ctx_start),
        grid=(n_rows // TM,),
        in_specs=in_specs,
        out_specs=out_specs,
        out_shape=out_shape,
        compiler_params=_cparams("parallel"),
        name="nm_matmul",
    )(*args)


def _out_proj_kernel(y1_ref, y2_ref, w1_ref, w2_ref, hl_ref, hc_ref, g_ref, gate_ref, o_ref, *, ctx_start):
    y = (jnp.dot(y1_ref[...], w1_ref[...], preferred_element_type=F32)
         + jnp.dot(y2_ref[...], w2_ref[...], preferred_element_type=F32))
    o_ref[...] = _gated_residual(_tile_h(hl_ref, hc_ref, ctx_start, pl.program_id(0)), y, g_ref, gate_ref)


def _out_proj(y1, y2, w_stack, w_layer, h, g, mods, layer, n_rows, n_lat, mod_row):
    half = y1.shape[1]
    h_specs, h_args, ctx_start = _h_operands(h, TM, n_lat, lambda i: i)
    return pl.pallas_call(
        functools.partial(_out_proj_kernel, ctx_start=ctx_start),
        grid=(n_rows // TM,),
        in_specs=[pl.BlockSpec((TM, half), lambda i: (i, 0)),
                  pl.BlockSpec((TM, half), lambda i: (i, 0)),
                  _layer_resident((half, D_MODEL), w_layer, 0),
                  _layer_resident((half, D_MODEL), w_layer, 1)] + h_specs + [
                  _resident((1, D_MODEL)),
                  _mod_spec(layer, 2, mod_row)],
        out_specs=pl.BlockSpec((TM, D_MODEL), lambda i: (i, 0)),
        out_shape=jax.ShapeDtypeStruct((n_rows, D_MODEL), F32),
        compiler_params=_cparams("parallel"),
        name="out_proj",
    )(y1, y2, w_stack, w_stack, *h_args, g, mods)


def _ffn_kernel(h_ref, g2_ref, sh_ref, sc_ref, wg0_ref, wu0_ref, wd0_ref, wg1_ref, wu1_ref, wd1_ref,
                g3_ref, gate_ref, o_ref, u_ref, *, n_chunks):
    j = pl.program_id(1)

    @pl.when(j == 0)
    def _():
        u_ref[...] = _norm_mod(h_ref[...], g2_ref, sh_ref, sc_ref)
        o_ref[...] = jnp.zeros_like(o_ref)

    def contribution(wg_ref, wu_ref, wd_ref):
        u = u_ref[...]
        a = jnp.dot(u, wg_ref[...], preferred_element_type=F32)
        b = jnp.dot(u, wu_ref[...], preferred_element_type=F32)
        hid = (a * jax.nn.sigmoid(a) * b).astype(BF16)
        return jnp.dot(hid, wd_ref[...], preferred_element_type=F32)

    @pl.when(j < n_chunks // 2)
    def _():
        o_ref[...] += contribution(wg0_ref, wu0_ref, wd0_ref) + contribution(wg1_ref, wu1_ref, wd1_ref)

    if n_chunks % 2:
        @pl.when(j == n_chunks // 2)
        def _():
            o_ref[...] += contribution(wg0_ref, wu0_ref, wd0_ref)

    @pl.when(j == pl.num_programs(1) - 1)
    def _():
        o_ref[...] = _gated_residual(h_ref[...], o_ref[...], g3_ref, gate_ref)


def _chunk_major(w, th):
    layers, d, hidden = w.shape
    return w.reshape(layers, d, hidden // th, th).transpose(0, 2, 1, 3)


def _ffn(h, g2, g3, mods, layer, wg, wu, wd, n_rows, mod_row):
    th = TH_FFN
    tm = TM_FFN
    n_chunks = wg.shape[1]

    def chunk_specs(which):
        def chunk(j):
            return jnp.minimum(2 * j + which, n_chunks - 1)
        return [pl.BlockSpec((None, None, D_MODEL, th), lambda i, j: (layer, chunk(j), 0, 0)),
                pl.BlockSpec((None, None, D_MODEL, th), lambda i, j: (layer, chunk(j), 0, 0)),
                pl.BlockSpec((None, th, D_MODEL), lambda i, j: (layer, chunk(j), 0))]

    return pl.pallas_call(
        functools.partial(_ffn_kernel, n_chunks=n_chunks),
        grid=(n_rows // tm, (n_chunks + 1) // 2),
        in_specs=[pl.BlockSpec((tm, D_MODEL), lambda i, j: (i, 0)),
                  _resident((1, D_MODEL)),
                  _mod_spec(layer, 3, mod_row), _mod_spec(layer, 4, mod_row)]
                 + chunk_specs(0) + chunk_specs(1)
                 + [_resident((1, D_MODEL)), _mod_spec(layer, 5, mod_row)],
        out_specs=pl.BlockSpec((tm, D_MODEL), lambda i, j: (i, 0)),
        out_shape=jax.ShapeDtypeStruct((n_rows, D_MODEL), F32),
        scratch_shapes=[pltpu.VMEM((tm, D_MODEL), BF16)],
        compiler_params=_cparams("parallel", "arbitrary"),
        name="ffn",
    )(h, g2, mods, mods, wg, wu, wd, wg, wu, wd, g3, mods)


def _conv_kernel(v_ref, gt_ref, vp_ref, gp_ref, vn_ref, gn_ref, w_ref, cb_ref, lg_ref, lb_ref, o_ref,
                 buf_ref, sh_ref, acc_ref, *, lat_tiles, tiles_per_seq):
    i = pl.program_id(0)
    tt = v_ref.shape[0]
    pos = i % tiles_per_seq
    is_lat = i < lat_tiles
    has_prev = jnp.logical_and(is_lat, pos != 0)
    has_next = jnp.logical_and(is_lat, pos != tiles_per_seq - 1)

    def glu(v, g):
        return v[...].astype(F32) * jax.nn.sigmoid(g[...].astype(F32))

    buf_ref[0:HALO, :] = jnp.where(has_prev, glu(vp_ref, gp_ref), 0.0)
    buf_ref[HALO:HALO + tt, :] = glu(v_ref, gt_ref)
    buf_ref[HALO + tt:, :] = jnp.where(has_next, glu(vn_ref, gn_ref), 0.0)

    ext = sh_ref.shape[1]
    off = HALO - CONV_K // 2
    for c in range(CONV_CH // LANE):
        lanes = slice(c * LANE, (c + 1) * LANE)
        col = buf_ref[:, lanes]
        for s in range(SUBLANE):
            sh_ref[s, :, lanes] = col[s:s + ext, :]
    for c in range(CONV_CH // LANE):
        lanes = slice(c * LANE, (c + 1) * LANE)
        acc = jnp.broadcast_to(cb_ref[:, lanes], (tt, LANE))
        for k in range(CONV_K):
            s = (k + off) % SUBLANE
            base = (k + off) - s
            acc = acc + w_ref[k:k + 1, lanes] * sh_ref[s, base:base + tt, lanes]
        acc_ref[:, lanes] = acc

    rc = 32
    for r in range(tt // rc):
        x = acc_ref[r * rc:(r + 1) * rc, :]
        mu = jnp.mean(x, axis=-1, keepdims=True)
        xc = x - mu
        var = jnp.mean(xc * xc, axis=-1, keepdims=True)
        y = xc * lax.rsqrt(var + LN_EPS) * lg_ref[...] + lb_ref[...]
        o_ref[r * rc:(r + 1) * rc, :] = (y * jax.nn.sigmoid(y)).astype(o_ref.dtype)


def _conv_branch(p, conv_w, conv_b, ln_g, ln_b, n_lat, seq_lat, seq_ctx, n_rows):
    tt = TT_CONV
    assert seq_ctx == tt and seq_lat % tt == 0
    hb = tt // HALO
    last_hb = n_rows // HALO - 1
    kern = functools.partial(_conv_kernel, lat_tiles=n_lat // tt, tiles_per_seq=seq_lat // tt)

    def prev_map(col):
        return lambda i: (jnp.maximum(i * hb - 1, 0), col)

    def next_map(col):
        return lambda i: (jnp.minimum((i + 1) * hb, last_hb), col)

    return pl.pallas_call(
        kern,
        grid=(n_rows // tt,),
        in_specs=[pl.BlockSpec((tt, CONV_CH), lambda i: (i, 0)),
                  pl.BlockSpec((tt, CONV_CH), lambda i: (i, 1)),
                  pl.BlockSpec((HALO, CONV_CH), prev_map(0)),
                  pl.BlockSpec((HALO, CONV_CH), prev_map(1)),
                  pl.BlockSpec((HALO, CONV_CH), next_map(0)),
                  pl.BlockSpec((HALO, CONV_CH), next_map(1)),
                  _resident((CONV_K, CONV_CH)),
                  _resident((1, CONV_CH)), _resident((1, CONV_CH)), _resident((1, CONV_CH))],
        out_specs=pl.BlockSpec((tt, CONV_CH), lambda i: (i, 0)),
        out_shape=jax.ShapeDtypeStruct((n_rows, CONV_CH), BF16),
        scratch_shapes=[pltpu.VMEM((tt + 2 * HALO, CONV_CH), F32),
                        pltpu.VMEM((SUBLANE, tt + 2 * HALO - SUBLANE, CONV_CH), F32),
                        pltpu.VMEM((tt, CONV_CH), F32)],
        compiler_params=_cparams("parallel"),
        name="conv_branch",
    )(p, p, p, p, p, p, conv_w, conv_b, ln_g, ln_b)


def _dft_pos_kernel(ct_ref, st_ref, zc_ref, zs_ref, ctx_ct_ref, ctx_st_ref, zcx_ref, zsx_ref, o_ref, *, lat_steps):
    def run(c_ref, s_ref, a_ref, b_ref):
        y = (jnp.dot(c_ref[...], a_ref[...], preferred_element_type=F32)
             + jnp.dot(s_ref[...], b_ref[...], preferred_element_type=F32))
        o_ref[...] = y.astype(o_ref.dtype)

    is_ctx = pl.program_id(1) >= lat_steps
    pl.when(is_ctx)(functools.partial(run, ctx_ct_ref, ctx_st_ref, zcx_ref, zsx_ref))
    pl.when(jnp.logical_not(is_ctx))(functools.partial(run, ct_ref, st_ref, zc_ref, zs_ref))


def _dft_positions(ct, nst, ct_ctx, nst_ctx, zc, zs, nbatch, seq_lat, seq_ctx):
    tk = TK_DFT
    assert seq_ctx == tk and seq_lat % tk == 0
    lat_steps = seq_lat // tk
    ctx_blk0 = nbatch * lat_steps

    def tab_map(b, k):
        return (jnp.minimum(k, lat_steps - 1), 0)

    def out_map(b, k):
        return (jnp.where(k < lat_steps, b * lat_steps + k, ctx_blk0 + b), 0)

    lat_tab = pl.BlockSpec((tk, seq_lat), tab_map)
    lat_z = pl.BlockSpec((seq_lat, FOURIER_CH), lambda b, k: (b, 0))
    ctx_z = pl.BlockSpec((seq_ctx, FOURIER_CH), lambda b, k: (ctx_blk0 + b, 0))
    return pl.pallas_call(
        functools.partial(_dft_pos_kernel, lat_steps=lat_steps),
        grid=(nbatch, lat_steps + 1),
        in_specs=[lat_tab, lat_tab, lat_z, lat_z,
                  _resident((seq_ctx, seq_ctx)), _resident((seq_ctx, seq_ctx)), ctx_z, ctx_z],
        out_specs=pl.BlockSpec((tk, FOURIER_CH), out_map),
        out_shape=jax.ShapeDtypeStruct((nbatch * (seq_lat + seq_ctx), FOURIER_CH), BF16),
        compiler_params=_cparams("parallel", "arbitrary"),
        name="dft_positions",
    )(ct, nst, zc, zs, ct_ctx, nst_ctx, zc, zs)


def _dft_tables(n, scale):
    k = jnp.arange(n, dtype=jnp.int32)
    m = (k[:, None] * k[None, :]) % n
    ang = m.astype(F32) * (2.0 * jnp.pi / n)
    return (jnp.cos(ang) * scale).astype(BF16), (-jnp.sin(ang) * scale).astype(BF16)


def _dft_tables_big(n, scale):
    r = int(round(n ** 0.5))
    assert r * r == n
    t = jnp.arange(n, dtype=jnp.int32)[None, :]
    kk = jnp.arange(r, dtype=jnp.int32)[:, None]
    a_hi = ((kk * r * t) % n).astype(F32) * (2.0 * jnp.pi / n)
    a_lo = ((kk * t) % n).astype(F32) * (2.0 * jnp.pi / n)
    ch, sh, cl, sl = jnp.cos(a_hi)[:, None, :], jnp.sin(a_hi)[:, None, :], jnp.cos(a_lo)[None], jnp.sin(a_lo)[None]
    c = (ch * cl - sh * sl) * scale
    s = (sh * cl + ch * sl) * scale
    return c.reshape(n, n).astype(BF16), (-s).reshape(n, n).astype(BF16)


def _rope(x, cos, sin_signed, first_half):
    swapped = jnp.where(first_half, pltpu.roll(x, LANE - 16, 1), pltpu.roll(x, 16, 1))
    return x * cos + swapped * sin_signed


def _odd_proj_kernel(p_ref, cos_ref, sin_ref, gq_ref, gkv_ref, wuq_ref, wuk_ref, wuv_ref,
                     q_ref, k_ref, v_ref, qs_ref, ks_ref, vs_ref):
    tm = p_ref.shape[0]
    cos = cos_ref[...]
    sin = sin_ref[...]
    lane = lax.broadcasted_iota(jnp.int32, (tm, LANE), 1)
    first_half = (lane % 32) < 16
    rope = functools.partial(_rope, cos=cos, sin_signed=sin, first_half=first_half)

    o_kv = Q_LORA
    o_kr = o_kv + KV_LORA
    o_qs = o_kr + LANE
    o_ks = o_qs + SWA_HEADS * SWA_HD
    o_vs = o_ks + SWA_KV_HEADS * SWA_HD

    nq = _rms(p_ref[:, 0:Q_LORA].astype(F32), gq_ref[...]).astype(BF16)
    nkv = _rms(p_ref[:, o_kv:o_kr].astype(F32), gkv_ref[...]).astype(BF16)
    q = jnp.dot(nq, wuq_ref[...], preferred_element_type=F32) * MLA_QSCALE
    kn = jnp.dot(nkv, wuk_ref[...], preferred_element_type=F32)
    v = jnp.dot(nkv, wuv_ref[...], preferred_element_type=F32)
    kr = rope(p_ref[:, o_kr:o_qs].astype(F32)).astype(k_ref.dtype)
    ones = jnp.ones((tm, LANE), v_ref.dtype)
    for h in range(MLA_HEADS):
        c0 = h * MLA_PAD
        q_ref[:, c0:c0 + LANE] = q[:, c0:c0 + LANE].astype(q_ref.dtype)
        q_ref[:, c0 + LANE:c0 + 2 * LANE] = rope(q[:, c0 + LANE:c0 + 2 * LANE]).astype(q_ref.dtype)
        k_ref[:, c0:c0 + LANE] = kn[:, h * MLA_NOPE:(h + 1) * MLA_NOPE].astype(k_ref.dtype)
        k_ref[:, c0 + LANE:c0 + 2 * LANE] = kr
        v_ref[:, c0:c0 + LANE] = v[:, h * MLA_V:(h + 1) * MLA_V].astype(v_ref.dtype)
        v_ref[:, c0 + LANE:c0 + 2 * LANE] = ones
    for j in range(SWA_HEADS * SWA_HD // LANE):
        x = p_ref[:, o_qs + j * LANE:o_qs + (j + 1) * LANE].astype(F32) * SWA_QSCALE
        qs_ref[:, j * LANE:(j + 1) * LANE] = rope(x).astype(qs_ref.dtype)
    ks_ref[...] = rope(p_ref[:, o_ks:o_vs].astype(F32)).astype(ks_ref.dtype)
    vs_ref[...] = p_ref[:, o_vs:o_vs + LANE]


def _odd_proj(p, cos_t, sin_t, gq, gkv, wuq, wuk, wuv, layer, n_lat, seq_lat, n_rows):
    tm = TM
    lat_tiles = n_lat // tm
    per_seq = seq_lat // tm

    def tab_map(i):
        return (jnp.where(i < lat_tiles, i % per_seq, per_seq), 0)

    def rows(w):
        return pl.BlockSpec((tm, w), lambda i: (i, 0))

    def out(w):
        return jax.ShapeDtypeStruct((n_rows, w), BF16)

    pad_w = MLA_HEADS * MLA_PAD
    return pl.pallas_call(
        _odd_proj_kernel,
        grid=(n_rows // tm,),
        in_specs=[rows(p.shape[1]),
                  pl.BlockSpec((tm, LANE), tab_map), pl.BlockSpec((tm, LANE), tab_map),
                  _resident((1, Q_LORA)), _resident((1, KV_LORA)),
                  _layer_resident(wuq.shape[1:], layer), _layer_resident(wuk.shape[1:], layer),
                  _layer_resident(wuv.shape[1:], layer)],
        out_specs=[rows(pad_w), rows(pad_w), rows(pad_w), rows(SWA_HEADS * SWA_HD), rows(LANE), rows(LANE)],
        out_shape=[out(pad_w), out(pad_w), out(pad_w), out(SWA_HEADS * SWA_HD), out(LANE), out(LANE)],
        compiler_params=_cparams("parallel"),
        name="odd_proj",
    )(p, cos_t, sin_t, gq, gkv, wuq, wuk, wuv)


def _rope_tables(seq, pad_rows):
    quarter = MLA_ROPE // 4
    inv = ROPE_BASE ** (-jnp.arange(quarter, dtype=F32) / quarter)
    t = jnp.arange(seq, dtype=jnp.int32)
    row = (t // GRID_W).astype(F32)[:, None] * inv
    col = (t % GRID_W).astype(F32)[:, None] * inv
    ang = jnp.concatenate([row, row, col, col], axis=-1)
    sign = jnp.concatenate([-jnp.ones((quarter,), F32), jnp.ones((quarter,), F32)] * 2)
    cos = jnp.cos(ang)
    sin = jnp.sin(ang) * sign
    cos = jnp.concatenate([jnp.tile(cos, (1, 2)), jnp.ones((pad_rows, LANE), F32)], axis=0)
    sin = jnp.concatenate([jnp.tile(sin, (1, 2)), jnp.zeros((pad_rows, LANE), F32)], axis=0)
    return cos, sin


def _qk(q, k):
    return lax.dot_general(q, k, (((1,), (1,)), ((), ())), preferred_element_type=F32)


def _pipeline_cases(i, lat_tiles, ctx_tiles, stage_a, stage_b):
    n = lat_tiles + ctx_tiles
    groups = {}
    for step in range(n + 1):
        has_a, has_b = step < n, step >= 1
        key = (has_a, has_a and step >= lat_tiles, has_b, has_b and step - 1 >= lat_tiles, step % 2)
        groups.setdefault(key, []).append(step)
    for (has_a, a_ctx, has_b, b_ctx, slot), steps in groups.items():
        def body(has_a=has_a, a_ctx=a_ctx, has_b=has_b, b_ctx=b_ctx, slot=slot):
            if has_a:
                stage_a(a_ctx, slot)
            if has_b:
                stage_b(b_ctx, 1 - slot)
        pl.when((i >= steps[0]) & (i <= steps[-1]) & (i % 2 == slot))(body)


def _mla_kernel(q_ref, kl_ref, kc_ref, vl_ref, vc_ref, o_ref, *scratch, heads, lat_tiles, ctx_tiles):
    i = pl.program_id(2)
    n_lat_keys = kl_ref.shape[0]

    def buffers(slot, head):
        base = 2 * (slot * heads + head)
        return scratch[base], scratch[base + 1]

    def stage_a(tile_is_ctx, slot):
        for head in range(heads):
            s_ref, m_ref = buffers(slot, head)
            cols = slice(head * MLA_PAD, (head + 1) * MLA_PAD)
            q = q_ref[:, cols]
            sc = _qk(q, kc_ref[:, cols])
            m = jnp.max(sc, axis=-1, keepdims=True)
            if not tile_is_ctx:
                sl = _qk(q, kl_ref[:, cols])
                m = jnp.maximum(m, jnp.max(sl, axis=-1, keepdims=True))
                s_ref[:, :n_lat_keys] = sl
            s_ref[:, n_lat_keys:] = sc
            m_ref[...] = jnp.broadcast_to(m, m_ref.shape)

    def stage_b(tile_is_ctx, slot):
        for head in range(heads):
            s_ref, m_ref = buffers(slot, head)
            cols = slice(head * MLA_PAD, (head + 1) * MLA_PAD)
            m = m_ref[:, 0:1]
            o = jnp.dot(jnp.exp2(s_ref[:, n_lat_keys:] - m).astype(BF16), vc_ref[:, cols],
                        preferred_element_type=F32)
            if not tile_is_ctx:
                o = o + jnp.dot(jnp.exp2(s_ref[:, :n_lat_keys] - m).astype(BF16), vl_ref[:, cols],
                                preferred_element_type=F32)
            o_ref[:, head * MLA_V:(head + 1) * MLA_V] = (o[:, :MLA_V] / o[:, MLA_V:]).astype(o_ref.dtype)

    _pipeline_cases(i, lat_tiles, ctx_tiles, stage_a, stage_b)


def _mla_attention(q, k, v, nbatch, seq_lat, seq_ctx, ctx_queries):
    tq = TQ_MLA
    n_lat = nbatch * seq_lat
    lat_tiles = seq_lat // tq
    ctx_tiles = seq_ctx // tq if ctx_queries else 0
    n_tiles = lat_tiles + ctx_tiles
    out_rows = n_lat + (nbatch * seq_ctx if ctx_queries else 0)
    ctx_blk0 = n_lat // seq_ctx

    def tile_row(b, t):
        return jnp.where(t < lat_tiles, b * lat_tiles + t, n_lat // tq + b * ctx_tiles + (t - lat_tiles))

    hp = MLA_HEADS_PER_STEP
    kern = functools.partial(_mla_kernel, heads=hp, lat_tiles=lat_tiles, ctx_tiles=ctx_tiles)
    n_keys = seq_lat + seq_ctx
    width = hp * MLA_PAD
    return pl.pallas_call(
        kern,
        grid=(nbatch, MLA_HEADS // hp, n_tiles + 1),
        in_specs=[pl.BlockSpec((tq, width), lambda b, h, i: (tile_row(b, jnp.minimum(i, n_tiles - 1)), h)),
                  pl.BlockSpec((seq_lat, width), lambda b, h, i: (b, h)),
                  pl.BlockSpec((seq_ctx, width), lambda b, h, i: (ctx_blk0 + b, h)),
                  pl.BlockSpec((seq_lat, width), lambda b, h, i: (b, h)),
                  pl.BlockSpec((seq_ctx, width), lambda b, h, i: (ctx_blk0 + b, h))],
        out_specs=pl.BlockSpec((tq, hp * MLA_V), lambda b, h, i: (tile_row(b, jnp.maximum(i - 1, 0)), h)),
        out_shape=jax.ShapeDtypeStruct((out_rows, MLA_HEADS * MLA_V), BF16),
        scratch_shapes=[pltpu.VMEM((tq, n_keys), F32), pltpu.VMEM((tq, LANE), F32)] * (2 * hp),
        compiler_params=_cparams("parallel", "parallel", "arbitrary"),
        name="mla_attention",
    )(q, k, k, v, v)


def _swa_kernel(sink_ref, q_ref, kp_ref, kc_ref, kn_ref, kx_ref, vp_ref, vc_ref, vn_ref, vx_ref, o_ref,
                s0_ref, m0_ref, s1_ref, m1_ref, *, lat_tiles, ctx_tiles, seq_lat):
    i = pl.program_id(1)
    nx = kx_ref.shape[0]
    lo = lax.broadcasted_iota(jnp.int32, (Q_BLOCK, LANE), 1) < LANE // 2
    slots = ((s0_ref, m0_ref), (s1_ref, m1_ref))

    def stage_a(tile_is_ctx, slot):
        s_ref, m_ref = slots[slot]
        zero = jnp.zeros((Q_BLOCK, LANE), q_ref.dtype)
        chunks = [q_ref[:, j * LANE:(j + 1) * LANE] for j in range(SWA_GROUP)]
        qstack = jnp.concatenate([jnp.where(lo, ch, zero) for ch in chunks]
                                 + [jnp.where(lo, zero, ch) for ch in chunks], axis=0)
        if tile_is_ctx:
            s = _qk(qstack, kx_ref[...])
            bias = None
        else:
            k_all = jnp.concatenate([kx_ref[...], kp_ref[...], kc_ref[...], kn_ref[...]], axis=0)
            nk = k_all.shape[0]
            qpos = i * Q_BLOCK + lax.broadcasted_iota(jnp.int32, (Q_BLOCK, nk), 0)
            col = lax.broadcasted_iota(jnp.int32, (Q_BLOCK, nk), 1)
            kpos = (i - 1) * Q_BLOCK + (col - nx)
            in_window = (kpos >= 0) & (kpos < seq_lat) & (jnp.abs(qpos - kpos) <= WINDOW)
            bias = jnp.where((col < nx) | in_window, 0.0, NEG_BIG)
            s = _qk(qstack, k_all)
        nk = s.shape[1]
        for r in range(SWA_HEADS):
            rows = slice(r * Q_BLOCK, (r + 1) * Q_BLOCK)
            blk = s[rows] if bias is None else s[rows] + bias
            s_ref[rows, :nk] = blk
            m = jnp.broadcast_to(jnp.max(blk, axis=-1, keepdims=True), (Q_BLOCK, LANE))
            m_ref[rows, :] = jnp.maximum(m, sink_ref[rows, :])

    def stage_b(tile_is_ctx, slot):
        s_ref, m_ref = slots[slot]
        m = m_ref[...]
        if tile_is_ctx:
            nk = nx
            v_all = vx_ref[...]
        else:
            nk = s_ref.shape[1]
            v_all = jnp.concatenate([vx_ref[...], vp_ref[...], vc_ref[...], vn_ref[...]], axis=0)
        v_ext = jnp.concatenate([v_all, jnp.ones_like(v_all)], axis=1)
        p = jnp.concatenate([jnp.exp2(s_ref[:, c * LANE:(c + 1) * LANE] - m).astype(BF16)
                             for c in range(nk // LANE)], axis=1)
        o = jnp.dot(p, v_ext, preferred_element_type=F32)
        o = o[:, :LANE] / (o[:, LANE:] + jnp.exp2(sink_ref[...] - m))
        for j in range(SWA_GROUP):
            pair = jnp.where(lo, o[j * Q_BLOCK:(j + 1) * Q_BLOCK],
                             o[(SWA_GROUP + j) * Q_BLOCK:(SWA_GROUP + j + 1) * Q_BLOCK])
            o_ref[:, j * LANE:(j + 1) * LANE] = pair.astype(o_ref.dtype)

    _pipeline_cases(i, lat_tiles, ctx_tiles, stage_a, stage_b)


def _swa_attention(sink_rows, qs, ks, vs, nbatch, seq_lat, seq_ctx, ctx_queries):
    qb = Q_BLOCK
    n_lat = nbatch * seq_lat
    lat_tiles = seq_lat // qb
    ctx_tiles = seq_ctx // qb if ctx_queries else 0
    n_tiles = lat_tiles + ctx_tiles
    out_rows = n_lat + (nbatch * seq_ctx if ctx_queries else 0)
    ctx_blk0 = n_lat // seq_ctx
    n_keys = seq_ctx + 3 * qb

    def tile_row(b, t):
        return jnp.where(t < lat_tiles, b * lat_tiles + t, n_lat // qb + b * ctx_tiles + (t - lat_tiles))

    def window(offset):
        def imap(b, i):
            return (b * lat_tiles + jnp.clip(i + offset, 0, lat_tiles - 1), 0)
        return pl.BlockSpec((qb, LANE), imap)

    xblk = pl.BlockSpec((seq_ctx, LANE), lambda b, i: (ctx_blk0 + b, 0))
    kern = functools.partial(_swa_kernel, lat_tiles=lat_tiles, ctx_tiles=ctx_tiles, seq_lat=seq_lat)
    width = SWA_HEADS * SWA_HD
    return pl.pallas_call(
        kern,
        grid=(nbatch, n_tiles + 1),
        in_specs=[_resident(sink_rows.shape),
                  pl.BlockSpec((qb, width), lambda b, i: (tile_row(b, jnp.minimum(i, n_tiles - 1)), 0)),
                  window(-1), window(0), window(1), xblk,
                  window(-2), window(-1), window(0), xblk],
        out_specs=pl.BlockSpec((qb, width), lambda b, i: (tile_row(b, jnp.maximum(i - 1, 0)), 0)),
        out_shape=jax.ShapeDtypeStruct((out_rows, width), BF16),
        scratch_shapes=[pltpu.VMEM((SWA_HEADS * qb, n_keys), F32), pltpu.VMEM((SWA_HEADS * qb, LANE), F32),
                        pltpu.VMEM((SWA_HEADS * qb, n_keys), F32), pltpu.VMEM((SWA_HEADS * qb, LANE), F32)],
        compiler_params=_cparams("parallel", "arbitrary"),
        name="swa_attention",
    )(sink_rows, qs, ks, ks, ks, ks, vs, vs, vs, vs)


def _pair_heads(w, axis):
    shape = w.shape
    w = w.reshape(shape[:axis] + (SWA_KV_HEADS, SWA_GROUP, SWA_HD) + shape[axis + 1:])
    w = jnp.swapaxes(w, axis, axis + 1)
    return w.reshape(shape)


def _odd_in_layout(w):
    o2 = Q_LORA + KV_LORA
    o3 = o2 + MLA_ROPE
    o4 = o3 + SWA_HEADS * SWA_HD
    kr = jnp.pad(w[..., o2:o3], ((0, 0), (0, 0), (0, LANE - MLA_ROPE)))
    return jnp.concatenate([w[..., :o2], kr, _pair_heads(w[..., o3:o4], 2), w[..., o4:]], axis=-1).astype(BF16)


def _uq_layout(w):
    per = MLA_NOPE + MLA_ROPE
    w = w.reshape(w.shape[:2] + (MLA_HEADS, per))
    w = jnp.pad(w, ((0, 0), (0, 0), (0, 0), (0, MLA_PAD - per)))
    return w.reshape(w.shape[:2] + (MLA_HEADS * MLA_PAD,)).astype(BF16)


def _out_odd_layout(w):
    half = w.shape[1] // 2
    return jnp.concatenate([w[:, :half], _pair_heads(w[:, half:], 1)], axis=1).astype(BF16)


def kernel(x, c, ctx, c_ctx, w_ada, b_ada, norm_g, w_in_even, conv_w, conv_b, conv_ln_g, conv_ln_b,
           w_in_odd, q_norm_g, kv_norm_g, w_uq, w_uk, w_uv, sink, w_out, w_gate, w_up, w_down):
    nbatch, seq, d = x.shape
    seq_ctx = ctx.shape[1]
    depth = w_ada.shape[0]
    n_lat = nbatch * seq
    n_all = n_lat + nbatch * seq_ctx

    def mod_row_for(tm):
        return lambda i: jnp.where(i < n_lat // tm, i // (seq // tm), nbatch)

    mod_row = mod_row_for(TM)

    cvec = jnp.concatenate([c, c_ctx[None, :], jnp.zeros((MOD_ROWS - nbatch - 1, d), F32)], axis=0)
    mods = _ada_mods(cvec, w_ada, b_ada).reshape(depth * MOD_ROWS * 6, 1, d)

    h = (x.reshape(n_lat, d), ctx.reshape(nbatch * seq_ctx, d))

    w_even = w_in_even.astype(BF16)
    w_odd = _odd_in_layout(w_in_odd)
    wuq, wuk, wuv = _uq_layout(w_uq), w_uk.astype(BF16), w_uv.astype(BF16)
    wo_even = w_out[0::2].astype(BF16)
    wo_odd = _out_odd_layout(w_out[1::2])
    wg = _chunk_major(w_gate.astype(BF16), TH_FFN)
    wu = _chunk_major(w_up.astype(BF16), TH_FFN)
    wd = w_down.astype(BF16)
    sink_rows = jnp.repeat(sink * LOG2E, Q_BLOCK, axis=1)[:, :, None]
    sink_rows = jnp.broadcast_to(sink_rows, sink_rows.shape[:2] + (LANE,))

    ortho = 1.0 / float((seq * FOURIER_GC) ** 0.5)
    ortho_ctx = 1.0 / float((seq_ctx * FOURIER_GC) ** 0.5)
    ct_lat, nst_lat = _dft_tables_big(seq, ortho)
    ct_ctx, nst_ctx = _dft_tables(seq_ctx, ortho_ctx)
    cc, ncs = _dft_tables(FOURIER_GC, 1.0)
    cs_c = jnp.concatenate([cc, -ncs], axis=1)
    cos_t, sin_t = _rope_tables(seq, TM)

    for l in range(depth):
        last = l == depth - 1
        j = l // 2
        g = norm_g[l][:, None, :]
        rows_out = n_lat if last else n_all
        if l % 2 == 0:
            p, zc, zs = _nm_matmul(h, g[0], mods, l, w_even, j, n_all, n_lat, mod_row, cs_c)
            y1 = _conv_branch(p, conv_w[j], conv_b[j][None], conv_ln_g[j][None], conv_ln_b[j][None],
                              n_lat, seq, seq_ctx, n_all)
            y2 = _dft_positions(ct_lat, nst_lat, ct_ctx, nst_ctx, zc, zs, nbatch, seq, seq_ctx)
            wo = wo_even
        else:
            p = _nm_matmul(h, g[0], mods, l, w_odd, j, n_all, n_lat, mod_row)
            q, k, v, qs, ks, vs = _odd_proj(p, cos_t, sin_t, q_norm_g[j][None], kv_norm_g[j][None],
                                            wuq, wuk, wuv, j, n_lat, seq, n_all)
            y1 = _mla_attention(q, k, v, nbatch, seq, seq_ctx, not last)
            y2 = _swa_attention(sink_rows[j], qs, ks, vs, nbatch, seq, seq_ctx, not last)
            wo = wo_odd
        h = _out_proj(y1, y2, wo, j, h, g[1], mods, l, rows_out, n_lat, mod_row)
        h = _ffn(h, g[2], g[3], mods, l, wg, wu, wd, rows_out, mod_row_for(TM_FFN))
    return h.reshape(nbatch, seq, d)
```

```python
import functools
import math

import jax
import jax.numpy as jnp
from jax import lax
from jax.experimental import pallas as pl
from jax.experimental.pallas import tpu as pltpu

F32 = jnp.float32
BF16 = jnp.bfloat16

D_MODEL = 2048
GRID_W = 64
CONV_CH = 1024
CONV_K = 31
FOURIER_CH = 1024
FOURIER_GROUPS = 4
FOURIER_GC = FOURIER_CH // FOURIER_GROUPS
MLA_NOPE = 128
MLA_ROPE = 64
MLA_V = 128
MLA_HEADS = 8
MLA_PAD = 256
Q_LORA = 512
KV_LORA = 512
SWA_HD = 64
SWA_HEADS = 16
SWA_KV_HEADS = 2
SWA_GROUP = SWA_HEADS // SWA_KV_HEADS
WINDOW = 128
Q_BLOCK = 128
ROPE_BASE = 10000.0
NORM_EPS = 1e-6
LN_EPS = 1e-5
LOG2E = math.log2(math.e)
MLA_QSCALE = (MLA_NOPE + MLA_ROPE) ** -0.5 * LOG2E
SWA_QSCALE = SWA_HD ** -0.5 * LOG2E
LANE = 128
SUBLANE = 8
HALO = 16
MOD_ROWS = 8
VMEM_LIMIT = 56 * 1024 * 1024

TM = 512
TM_FFN = 512
TH_FFN = 512
TQ_MLA = 256
MLA_HEADS_PER_STEP = 2
TT_CONV = 256
TK_DFT = 256
NEG_BIG = -1e30


def _cparams(*sem):
    return pltpu.CompilerParams(dimension_semantics=sem, vmem_limit_bytes=VMEM_LIMIT)


def _resident(shape):
    nd = len(shape)
    return pl.BlockSpec(shape, lambda *_: (0,) * nd, pipeline_mode=pl.Buffered(1))


def _layer_resident(shape, layer, block=0):
    return pl.BlockSpec((None,) + tuple(shape), lambda *_: (layer, block, 0), pipeline_mode=pl.Buffered(1))


def _rms_scale(x):
    return lax.rsqrt(jnp.mean(x * x, axis=-1, keepdims=True) + NORM_EPS)


def _rms(x, g):
    return x * _rms_scale(x) * g


def _norm_mod(x, g_ref, sh_ref, sc_ref):
    gain = g_ref[...] * (1.0 + sc_ref[...])
    return (x * _rms_scale(x) * gain + sh_ref[...]).astype(BF16)


def _gated_residual(h, y, g_ref, gate_ref):
    return h + y * _rms_scale(y) * (gate_ref[...] * g_ref[...])


def _h_operands(h, tm, n_lat):
    if isinstance(h, tuple):
        lat_tiles = n_lat // tm
        specs = [pl.BlockSpec((tm, D_MODEL), lambda i: (jnp.minimum(i, lat_tiles - 1), 0)),
                 pl.BlockSpec((tm, D_MODEL), lambda i: (jnp.maximum(i - lat_tiles, 0), 0))]
        return specs, list(h), lat_tiles
    specs = [pl.BlockSpec((tm, D_MODEL), lambda i: (i, 0)), _resident((tm, D_MODEL))]
    return specs, [h, h], None


def _tile_h(hl_ref, hc_ref, ctx_start):
    if ctx_start is None:
        return hl_ref[...]
    return jnp.where(pl.program_id(0) >= ctx_start, hc_ref[...], hl_ref[...])


def _mod_spec(layer, chunk, mod_row_of_tile):
    def imap(i, *_):
        return ((layer * MOD_ROWS + mod_row_of_tile(i)) * 6 + chunk, 0, 0)
    return pl.BlockSpec((None, 1, D_MODEL), imap)


def _ada_kernel(c_ref, w_ref, b_ref, o_ref):
    c = c_ref[...]
    a = (c * jax.nn.sigmoid(c)).astype(BF16)
    o_ref[...] = jnp.dot(a, w_ref[...].astype(BF16), preferred_element_type=F32) + b_ref[...]


def _ada_mods(cvec, w_ada, b_ada):
    depth, d, n6 = w_ada.shape
    tn = 1024
    return pl.pallas_call(
        _ada_kernel,
        grid=(depth, n6 // tn),
        in_specs=[pl.BlockSpec((MOD_ROWS, d), lambda l, j: (0, 0)),
                  pl.BlockSpec((None, d, tn), lambda l, j: (l, 0, j)),
                  pl.BlockSpec((None, 1, tn), lambda l, j: (l, 0, j))],
        out_specs=pl.BlockSpec((None, MOD_ROWS, tn), lambda l, j: (l, 0, j)),
        out_shape=jax.ShapeDtypeStruct((depth, MOD_ROWS, n6), F32),
        compiler_params=_cparams("parallel", "parallel"),
        name="ada_mods",
    )(cvec, w_ada, b_ada.reshape(depth, 1, n6))


def _nm_matmul_kernel(hl_ref, hc_ref, g_ref, sh_ref, sc_ref, w_ref, o_ref, *, ctx_start):
    u = _norm_mod(_tile_h(hl_ref, hc_ref, ctx_start), g_ref, sh_ref, sc_ref)
    o_ref[...] = jnp.dot(u, w_ref[...], preferred_element_type=F32).astype(o_ref.dtype)


def _nm_matmul_dft_kernel(hl_ref, hc_ref, g_ref, sh_ref, sc_ref, w_ref, cs_ref, o_ref, zc_ref, zs_ref,
                          *, ctx_start):
    u = _norm_mod(_tile_h(hl_ref, hc_ref, ctx_start), g_ref, sh_ref, sc_ref)
    y = jnp.dot(u, w_ref[...], preferred_element_type=F32)
    n_conv = o_ref.shape[1]
    o_ref[...] = y[:, :n_conv].astype(o_ref.dtype)
    for grp in range(FOURIER_GROUPS):
        cols = slice(grp * FOURIER_GC, (grp + 1) * FOURIER_GC)
        f = y[:, n_conv + grp * FOURIER_GC:n_conv + (grp + 1) * FOURIER_GC].astype(BF16)
        z = jnp.dot(f, cs_ref[...], preferred_element_type=F32)
        zc_ref[:, cols] = z[:, :FOURIER_GC].astype(zc_ref.dtype)
        zs_ref[:, cols] = z[:, FOURIER_GC:].astype(zs_ref.dtype)


def _nm_matmul(h, g, mods, layer, w_stack, w_layer, n_rows, n_lat, mod_row, cs_c=None):
    nout = w_stack.shape[2]

    def rows(w):
        return pl.BlockSpec((TM, w), lambda i: (i, 0))

    def out(w):
        return jax.ShapeDtypeStruct((n_rows, w), BF16)

    h_specs, h_args, ctx_start = _h_operands(h, TM, n_lat)
    in_specs = h_specs + [_resident((1, D_MODEL)),
                          _mod_spec(layer, 0, mod_row), _mod_spec(layer, 1, mod_row),
                          _layer_resident((D_MODEL, nout), w_layer)]
    args = h_args + [g, mods, mods, w_stack]
    if cs_c is None:
        kern, out_specs, out_shape = _nm_matmul_kernel, rows(nout), out(nout)
    else:
        n_conv = nout - FOURIER_CH
        kern = _nm_matmul_dft_kernel
        in_specs.append(_resident(cs_c.shape))
        args.append(cs_c)
        out_specs = [rows(n_conv), rows(FOURIER_CH), rows(FOURIER_CH)]
        out_shape = [out(n_conv), out(FOURIER_CH), out(FOURIER_CH)]
    return pl.pallas_call(
        functools.partial(kern, ctx_start=ctx_start),
        grid=(n_rows // TM,),
        in_specs=in_specs,
        out_specs=out_specs,
        out_shape=out_shape,
        compiler_params=_cparams("parallel"),
        name="nm_matmul",
    )(*args)


def _out_proj_kernel(y1_ref, y2_ref, w1_ref, w2_ref, hl_ref, hc_ref, g_ref, gate_ref, o_ref, *, ctx_start):
    y = (jnp.dot(y1_ref[...], w1_ref[...], preferred_element_type=F32)
         + jnp.dot(y2_ref[...], w2_ref[...], preferred_element_type=F32))
    o_ref[...] = _gated_residual(_tile_h(hl_ref, hc_ref, ctx_start), y, g_ref, gate_ref)


def _out_proj(y1, y2, w_stack, w_layer, h, g, mods, layer, n_rows, n_lat, mod_row):
    half = y1.shape[1]
    h_specs, h_args, ctx_start = _h_operands(h, TM, n_lat)
    return pl.pallas_call(
        functools.partial(_out_proj_kernel, ctx_start=ctx_start),
        grid=(n_rows // TM,),
        in_specs=[pl.BlockSpec((TM, half), lambda i: (i, 0)),
                  pl.BlockSpec((TM, half), lambda i: (i, 0)),
                  _layer_resident((half, D_MODEL), w_layer, 0),
                  _layer_resident((half, D_MODEL), w_layer, 1)] + h_specs + [
                  _resident((1, D_MODEL)),
                  _mod_spec(layer, 2, mod_row)],
        out_specs=pl.BlockSpec((TM, D_MODEL), lambda i: (i, 0)),
        out_shape=jax.ShapeDtypeStruct((n_rows, D_MODEL), F32),
        compiler_params=_cparams("parallel"),
        name="out_proj",
    )(y1, y2, w_stack, w_stack, *h_args, g, mods)


def _ffn_kernel(h_ref, g2_ref, sh_ref, sc_ref, wgu_ref, wd_ref, g3_ref, gate_ref, o_ref, u_ref):
    j = pl.program_id(1)

    @pl.when(j == 0)
    def _():
        u_ref[...] = _norm_mod(h_ref[...], g2_ref, sh_ref, sc_ref)
        o_ref[...] = jnp.zeros_like(o_ref)

    th = wd_ref.shape[0]
    ab = jnp.dot(u_ref[...], wgu_ref[...], preferred_element_type=F32)
    a = ab[:, :th]
    b = ab[:, th:]
    hid = (a * jax.nn.sigmoid(a) * b).astype(BF16)
    o_ref[...] += jnp.dot(hid, wd_ref[...], preferred_element_type=F32)

    @pl.when(j == pl.num_programs(1) - 1)
    def _():
        o_ref[...] = _gated_residual(h_ref[...], o_ref[...], g3_ref, gate_ref)


def _gate_up_layout(w_gate, w_up, th):
    layers, d, hidden = w_gate.shape
    pair = jnp.concatenate([w_gate.reshape(layers, d, hidden // th, th),
                            w_up.reshape(layers, d, hidden // th, th)], axis=3)
    return pair.astype(BF16).reshape(layers, d, 2 * hidden)


def _ffn(h, g2, g3, mods, layer, wgu, wd, n_rows, mod_row):
    hidden = wd.shape[1]
    th = TH_FFN
    tm = TM_FFN
    return pl.pallas_call(
        _ffn_kernel,
        grid=(n_rows // tm, hidden // th),
        in_specs=[pl.BlockSpec((tm, D_MODEL), lambda i, j: (i, 0)),
                  _resident((1, D_MODEL)),
                  _mod_spec(layer, 3, mod_row), _mod_spec(layer, 4, mod_row),
                  pl.BlockSpec((None, D_MODEL, 2 * th), lambda i, j: (layer, 0, j)),
                  pl.BlockSpec((None, th, D_MODEL), lambda i, j: (layer, j, 0)),
                  _resident((1, D_MODEL)),
                  _mod_spec(layer, 5, mod_row)],
        out_specs=pl.BlockSpec((tm, D_MODEL), lambda i, j: (i, 0)),
        out_shape=jax.ShapeDtypeStruct((n_rows, D_MODEL), F32),
        scratch_shapes=[pltpu.VMEM((tm, D_MODEL), BF16)],
        compiler_params=_cparams("parallel", "arbitrary"),
        name="ffn",
    )(h, g2, mods, mods, wgu, wd, g3, mods)


def _conv_kernel(v_ref, gt_ref, vp_ref, gp_ref, vn_ref, gn_ref, w_ref, cb_ref, lg_ref, lb_ref, o_ref,
                 buf_ref, sh_ref, acc_ref, *, lat_tiles, tiles_per_seq):
    i = pl.program_id(0)
    tt = v_ref.shape[0]
    pos = i % tiles_per_seq
    is_lat = i < lat_tiles
    has_prev = jnp.logical_and(is_lat, pos != 0)
    has_next = jnp.logical_and(is_lat, pos != tiles_per_seq - 1)

    def glu(v, g):
        return v[...].astype(F32) * jax.nn.sigmoid(g[...].astype(F32))

    buf_ref[0:HALO, :] = jnp.where(has_prev, glu(vp_ref, gp_ref), 0.0)
    buf_ref[HALO:HALO + tt, :] = glu(v_ref, gt_ref)
    buf_ref[HALO + tt:, :] = jnp.where(has_next, glu(vn_ref, gn_ref), 0.0)

    ext = sh_ref.shape[1]
    off = HALO - CONV_K // 2
    for c in range(CONV_CH // LANE):
        lanes = slice(c * LANE, (c + 1) * LANE)
        col = buf_ref[:, lanes]
        for s in range(SUBLANE):
            sh_ref[s, :, lanes] = col[s:s + ext, :]
    for c in range(CONV_CH // LANE):
        lanes = slice(c * LANE, (c + 1) * LANE)
        acc = jnp.broadcast_to(cb_ref[:, lanes], (tt, LANE))
        for k in range(CONV_K):
            s = (k + off) % SUBLANE
            base = (k + off) - s
            acc = acc + w_ref[k:k + 1, lanes] * sh_ref[s, base:base + tt, lanes]
        acc_ref[:, lanes] = acc

    rc = 32
    for r in range(tt // rc):
        x = acc_ref[r * rc:(r + 1) * rc, :]
        mu = jnp.mean(x, axis=-1, keepdims=True)
        xc = x - mu
        var = jnp.mean(xc * xc, axis=-1, keepdims=True)
        y = xc * lax.rsqrt(var + LN_EPS) * lg_ref[...] + lb_ref[...]
        o_ref[r * rc:(r + 1) * rc, :] = (y * jax.nn.sigmoid(y)).astype(o_ref.dtype)


def _conv_branch(p, conv_w, conv_b, ln_g, ln_b, n_lat, seq_lat, seq_ctx, n_rows):
    tt = TT_CONV
    assert seq_ctx == tt and seq_lat % tt == 0
    hb = tt // HALO
    last_hb = n_rows // HALO - 1
    kern = functools.partial(_conv_kernel, lat_tiles=n_lat // tt, tiles_per_seq=seq_lat // tt)

    def prev_map(col):
        return lambda i: (jnp.maximum(i * hb - 1, 0), col)

    def next_map(col):
        return lambda i: (jnp.minimum((i + 1) * hb, last_hb), col)

    return pl.pallas_call(
        kern,
        grid=(n_rows // tt,),
        in_specs=[pl.BlockSpec((tt, CONV_CH), lambda i: (i, 0)),
                  pl.BlockSpec((tt, CONV_CH), lambda i: (i, 1)),
                  pl.BlockSpec((HALO, CONV_CH), prev_map(0)),
                  pl.BlockSpec((HALO, CONV_CH), prev_map(1)),
                  pl.BlockSpec((HALO, CONV_CH), next_map(0)),
                  pl.BlockSpec((HALO, CONV_CH), next_map(1)),
                  _resident((CONV_K, CONV_CH)),
                  _resident((1, CONV_CH)), _resident((1, CONV_CH)), _resident((1, CONV_CH))],
        out_specs=pl.BlockSpec((tt, CONV_CH), lambda i: (i, 0)),
        out_shape=jax.ShapeDtypeStruct((n_rows, CONV_CH), BF16),
        scratch_shapes=[pltpu.VMEM((tt + 2 * HALO, CONV_CH), F32),
                        pltpu.VMEM((SUBLANE, tt + 2 * HALO - SUBLANE, CONV_CH), F32),
                        pltpu.VMEM((tt, CONV_CH), F32)],
        compiler_params=_cparams("parallel"),
        name="conv_branch",
    )(p, p, p, p, p, p, conv_w, conv_b, ln_g, ln_b)


def _dft_pos_kernel(ct_ref, st_ref, zc_ref, zs_ref, ctx_ct_ref, ctx_st_ref, zcx_ref, zsx_ref, o_ref, *, lat_steps):
    def run(c_ref, s_ref, a_ref, b_ref):
        y = (jnp.dot(c_ref[...], a_ref[...], preferred_element_type=F32)
             + jnp.dot(s_ref[...], b_ref[...], preferred_element_type=F32))
        o_ref[...] = y.astype(o_ref.dtype)

    is_ctx = pl.program_id(1) >= lat_steps
    pl.when(is_ctx)(functools.partial(run, ctx_ct_ref, ctx_st_ref, zcx_ref, zsx_ref))
    pl.when(jnp.logical_not(is_ctx))(functools.partial(run, ct_ref, st_ref, zc_ref, zs_ref))


def _dft_positions(ct, nst, ct_ctx, nst_ctx, zc, zs, nbatch, seq_lat, seq_ctx):
    tk = TK_DFT
    assert seq_ctx == tk and seq_lat % tk == 0
    lat_steps = seq_lat // tk
    ctx_blk0 = nbatch * lat_steps

    def tab_map(b, k):
        return (jnp.minimum(k, lat_steps - 1), 0)

    def out_map(b, k):
        return (jnp.where(k < lat_steps, b * lat_steps + k, ctx_blk0 + b), 0)

    lat_tab = pl.BlockSpec((tk, seq_lat), tab_map)
    lat_z = pl.BlockSpec((seq_lat, FOURIER_CH), lambda b, k: (b, 0))
    ctx_z = pl.BlockSpec((seq_ctx, FOURIER_CH), lambda b, k: (ctx_blk0 + b, 0))
    return pl.pallas_call(
        functools.partial(_dft_pos_kernel, lat_steps=lat_steps),
        grid=(nbatch, lat_steps + 1),
        in_specs=[lat_tab, lat_tab, lat_z, lat_z,
                  _resident((seq_ctx, seq_ctx)), _resident((seq_ctx, seq_ctx)), ctx_z, ctx_z],
        out_specs=pl.BlockSpec((tk, FOURIER_CH), out_map),
        out_shape=jax.ShapeDtypeStruct((nbatch * (seq_lat + seq_ctx), FOURIER_CH), BF16),
        compiler_params=_cparams("parallel", "arbitrary"),
        name="dft_positions",
    )(ct, nst, zc, zs, ct_ctx, nst_ctx, zc, zs)


def _dft_tables(n, scale):
    k = jnp.arange(n, dtype=jnp.int32)
    m = (k[:, None] * k[None, :]) % n
    ang = m.astype(F32) * (2.0 * jnp.pi / n)
    return (jnp.cos(ang) * scale).astype(BF16), (-jnp.sin(ang) * scale).astype(BF16)


def _dft_tables_big(n, scale):
    r = int(round(n ** 0.5))
    assert r * r == n
    t = jnp.arange(n, dtype=jnp.int32)[None, :]
    kk = jnp.arange(r, dtype=jnp.int32)[:, None]
    a_hi = ((kk * r * t) % n).astype(F32) * (2.0 * jnp.pi / n)
    a_lo = ((kk * t) % n).astype(F32) * (2.0 * jnp.pi / n)
    ch, sh, cl, sl = jnp.cos(a_hi)[:, None, :], jnp.sin(a_hi)[:, None, :], jnp.cos(a_lo)[None], jnp.sin(a_lo)[None]
    c = (ch * cl - sh * sl) * scale
    s = (sh * cl + ch * sl) * scale
    return c.reshape(n, n).astype(BF16), (-s).reshape(n, n).astype(BF16)


def _rope(x, cos, sin_signed, first_half):
    swapped = jnp.where(first_half, pltpu.roll(x, LANE - 16, 1), pltpu.roll(x, 16, 1))
    return x * cos + swapped * sin_signed


def _odd_proj_kernel(p_ref, cos_ref, sin_ref, gq_ref, gkv_ref, wuq_ref, wuk_ref, wuv_ref,
                     q_ref, k_ref, v_ref, qs_ref, ks_ref, vs_ref):
    tm = p_ref.shape[0]
    cos = cos_ref[...]
    sin = sin_ref[...]
    lane = lax.broadcasted_iota(jnp.int32, (tm, LANE), 1)
    first_half = (lane % 32) < 16
    rope = functools.partial(_rope, cos=cos, sin_signed=sin, first_half=first_half)

    o_kv = Q_LORA
    o_kr = o_kv + KV_LORA
    o_qs = o_kr + LANE
    o_ks = o_qs + SWA_HEADS * SWA_HD
    o_vs = o_ks + SWA_KV_HEADS * SWA_HD

    nq = _rms(p_ref[:, 0:Q_LORA].astype(F32), gq_ref[...]).astype(BF16)
    nkv = _rms(p_ref[:, o_kv:o_kr].astype(F32), gkv_ref[...]).astype(BF16)
    q = jnp.dot(nq, wuq_ref[...], preferred_element_type=F32) * MLA_QSCALE
    kn = jnp.dot(nkv, wuk_ref[...], preferred_element_type=F32)
    v = jnp.dot(nkv, wuv_ref[...], preferred_element_type=F32)
    kr = rope(p_ref[:, o_kr:o_qs].astype(F32)).astype(k_ref.dtype)
    ones = jnp.ones((tm, LANE), v_ref.dtype)
    for h in range(MLA_HEADS):
        c0 = h * MLA_PAD
        q_ref[:, c0:c0 + LANE] = q[:, c0:c0 + LANE].astype(q_ref.dtype)
        q_ref[:, c0 + LANE:c0 + 2 * LANE] = rope(q[:, c0 + LANE:c0 + 2 * LANE]).astype(q_ref.dtype)
        k_ref[:, c0:c0 + LANE] = kn[:, h * MLA_NOPE:(h + 1) * MLA_NOPE].astype(k_ref.dtype)
        k_ref[:, c0 + LANE:c0 + 2 * LANE] = kr
        v_ref[:, c0:c0 + LANE] = v[:, h * MLA_V:(h + 1) * MLA_V].astype(v_ref.dtype)
        v_ref[:, c0 + LANE:c0 + 2 * LANE] = ones
    for j in range(SWA_HEADS * SWA_HD // LANE):
        x = p_ref[:, o_qs + j * LANE:o_qs + (j + 1) * LANE].astype(F32) * SWA_QSCALE
        qs_ref[:, j * LANE:(j + 1) * LANE] = rope(x).astype(qs_ref.dtype)
    ks_ref[...] = rope(p_ref[:, o_ks:o_vs].astype(F32)).astype(ks_ref.dtype)
    vs_ref[...] = p_ref[:, o_vs:o_vs + LANE]


def _odd_proj(p, cos_t, sin_t, gq, gkv, wuq, wuk, wuv, layer, n_lat, seq_lat, n_rows):
    tm = TM
    lat_tiles = n_lat // tm
    per_seq = seq_lat // tm

    def tab_map(i):
        return (jnp.where(i < lat_tiles, i % per_seq, per_seq), 0)

    def rows(w):
        return pl.BlockSpec((tm, w), lambda i: (i, 0))

    def out(w):
        return jax.ShapeDtypeStruct((n_rows, w), BF16)

    pad_w = MLA_HEADS * MLA_PAD
    return pl.pallas_call(
        _odd_proj_kernel,
        grid=(n_rows // tm,),
        in_specs=[rows(p.shape[1]),
                  pl.BlockSpec((tm, LANE), tab_map), pl.BlockSpec((tm, LANE), tab_map),
                  _resident((1, Q_LORA)), _resident((1, KV_LORA)),
                  _layer_resident(wuq.shape[1:], layer), _layer_resident(wuk.shape[1:], layer),
                  _layer_resident(wuv.shape[1:], layer)],
        out_specs=[rows(pad_w), rows(pad_w), rows(pad_w), rows(SWA_HEADS * SWA_HD), rows(LANE), rows(LANE)],
        out_shape=[out(pad_w), out(pad_w), out(pad_w), out(SWA_HEADS * SWA_HD), out(LANE), out(LANE)],
        compiler_params=_cparams("parallel"),
        name="odd_proj",
    )(p, cos_t, sin_t, gq, gkv, wuq, wuk, wuv)


def _rope_tables(seq, pad_rows):
    quarter = MLA_ROPE // 4
    inv = ROPE_BASE ** (-jnp.arange(quarter, dtype=F32) / quarter)
    t = jnp.arange(seq, dtype=jnp.int32)
    row = (t // GRID_W).astype(F32)[:, None] * inv
    col = (t % GRID_W).astype(F32)[:, None] * inv
    ang = jnp.concatenate([row, row, col, col], axis=-1)
    sign = jnp.concatenate([-jnp.ones((quarter,), F32), jnp.ones((quarter,), F32)] * 2)
    cos = jnp.cos(ang)
    sin = jnp.sin(ang) * sign
    cos = jnp.concatenate([jnp.tile(cos, (1, 2)), jnp.ones((pad_rows, LANE), F32)], axis=0)
    sin = jnp.concatenate([jnp.tile(sin, (1, 2)), jnp.zeros((pad_rows, LANE), F32)], axis=0)
    return cos, sin


def _qk(q, k):
    return lax.dot_general(q, k, (((1,), (1,)), ((), ())), preferred_element_type=F32)


def _pipeline_cases(i, lat_tiles, ctx_tiles, stage_a, stage_b):
    n = lat_tiles + ctx_tiles
    groups = {}
    for step in range(n + 1):
        has_a, has_b = step < n, step >= 1
        key = (has_a, has_a and step >= lat_tiles, has_b, has_b and step - 1 >= lat_tiles, step % 2)
        groups.setdefault(key, []).append(step)
    for (has_a, a_ctx, has_b, b_ctx, slot), steps in groups.items():
        def body(has_a=has_a, a_ctx=a_ctx, has_b=has_b, b_ctx=b_ctx, slot=slot):
            if has_a:
                stage_a(a_ctx, slot)
            if has_b:
                stage_b(b_ctx, 1 - slot)
        pl.when((i >= steps[0]) & (i <= steps[-1]) & (i % 2 == slot))(body)


def _mla_kernel(q_ref, kl_ref, kc_ref, vl_ref, vc_ref, o_ref, *scratch, heads, lat_tiles, ctx_tiles):
    i = pl.program_id(2)
    n_lat_keys = kl_ref.shape[0]

    def buffers(slot, head):
        base = 2 * (slot * heads + head)
        return scratch[base], scratch[base + 1]

    def stage_a(tile_is_ctx, slot):
        for head in range(heads):
            s_ref, m_ref = buffers(slot, head)
            cols = slice(head * MLA_PAD, (head + 1) * MLA_PAD)
            q = q_ref[:, cols]
            sc = _qk(q, kc_ref[:, cols])
            m = jnp.max(sc, axis=-1, keepdims=True)
            if not tile_is_ctx:
                sl = _qk(q, kl_ref[:, cols])
                m = jnp.maximum(m, jnp.max(sl, axis=-1, keepdims=True))
                s_ref[:, :n_lat_keys] = sl
            s_ref[:, n_lat_keys:] = sc
            m_ref[...] = jnp.broadcast_to(m, m_ref.shape)

    def stage_b(tile_is_ctx, slot):
        for head in range(heads):
            s_ref, m_ref = buffers(slot, head)
            cols = slice(head * MLA_PAD, (head + 1) * MLA_PAD)
            m = m_ref[:, 0:1]
            o = jnp.dot(jnp.exp2(s_ref[:, n_lat_keys:] - m).astype(BF16), vc_ref[:, cols],
                        preferred_element_type=F32)
            if not tile_is_ctx:
                o = o + jnp.dot(jnp.exp2(s_ref[:, :n_lat_keys] - m).astype(BF16), vl_ref[:, cols],
                                preferred_element_type=F32)
            o_ref[:, head * MLA_V:(head + 1) * MLA_V] = (o[:, :MLA_V] / o[:, MLA_V:]).astype(o_ref.dtype)

    _pipeline_cases(i, lat_tiles, ctx_tiles, stage_a, stage_b)


def _mla_attention(q, k, v, nbatch, seq_lat, seq_ctx, ctx_queries):
    tq = TQ_MLA
    n_lat = nbatch * seq_lat
    lat_tiles = seq_lat // tq
    ctx_tiles = seq_ctx // tq if ctx_queries else 0
    n_tiles = lat_tiles + ctx_tiles
    out_rows = n_lat + (nbatch * seq_ctx if ctx_queries else 0)
    ctx_blk0 = n_lat // seq_ctx

    def tile_row(b, t):
        return jnp.where(t < lat_tiles, b * lat_tiles + t, n_lat // tq + b * ctx_tiles + (t - lat_tiles))

    hp = MLA_HEADS_PER_STEP
    kern = functools.partial(_mla_kernel, heads=hp, lat_tiles=lat_tiles, ctx_tiles=ctx_tiles)
    n_keys = seq_lat + seq_ctx
    width = hp * MLA_PAD
    return pl.pallas_call(
        kern,
        grid=(nbatch, MLA_HEADS // hp, n_tiles + 1),
        in_specs=[pl.BlockSpec((tq, width), lambda b, h, i: (tile_row(b, jnp.minimum(i, n_tiles - 1)), h)),
                  pl.BlockSpec((seq_lat, width), lambda b, h, i: (b, h)),
                  pl.BlockSpec((seq_ctx, width), lambda b, h, i: (ctx_blk0 + b, h)),
                  pl.BlockSpec((seq_lat, width), lambda b, h, i: (b, h)),
                  pl.BlockSpec((seq_ctx, width), lambda b, h, i: (ctx_blk0 + b, h))],
        out_specs=pl.BlockSpec((tq, hp * MLA_V), lambda b, h, i: (tile_row(b, jnp.maximum(i - 1, 0)), h)),
        out_shape=jax.ShapeDtypeStruct((out_rows, MLA_HEADS * MLA_V), BF16),
        scratch_shapes=[pltpu.VMEM((tq, n_keys), F32), pltpu.VMEM((tq, LANE), F32)] * (2 * hp),
        compiler_params=_cparams("parallel", "parallel", "arbitrary"),
        name="mla_attention",
    )(q, k, k, v, v)


def _swa_kernel(sink_ref, q_ref, kp_ref, kc_ref, kn_ref, kx_ref, vp_ref, vc_ref, vn_ref, vx_ref, o_ref,
                s0_ref, m0_ref, s1_ref, m1_ref, *, lat_tiles, ctx_tiles, seq_lat):
    i = pl.program_id(1)
    nx = kx_ref.shape[0]
    lo = lax.broadcasted_iota(jnp.int32, (Q_BLOCK, LANE), 1) < LANE // 2
    slots = ((s0_ref, m0_ref), (s1_ref, m1_ref))

    def stage_a(tile_is_ctx, slot):
        s_ref, m_ref = slots[slot]
        zero = jnp.zeros((Q_BLOCK, LANE), q_ref.dtype)
        chunks = [q_ref[:, j * LANE:(j + 1) * LANE] for j in range(SWA_GROUP)]
        qstack = jnp.concatenate([jnp.where(lo, ch, zero) for ch in chunks]
                                 + [jnp.where(lo, zero, ch) for ch in chunks], axis=0)
        if tile_is_ctx:
            s = _qk(qstack, kx_ref[...])
            bias = None
        else:
            k_all = jnp.concatenate([kx_ref[...], kp_ref[...], kc_ref[...], kn_ref[...]], axis=0)
            nk = k_all.shape[0]
            qpos = i * Q_BLOCK + lax.broadcasted_iota(jnp.int32, (Q_BLOCK, nk), 0)
            col = lax.broadcasted_iota(jnp.int32, (Q_BLOCK, nk), 1)
            kpos = (i - 1) * Q_BLOCK + (col - nx)
            in_window = (kpos >= 0) & (kpos < seq_lat) & (jnp.abs(qpos - kpos) <= WINDOW)
            bias = jnp.where((col < nx) | in_window, 0.0, NEG_BIG)
            s = _qk(qstack, k_all)
        nk = s.shape[1]
        for r in range(SWA_HEADS):
            rows = slice(r * Q_BLOCK, (r + 1) * Q_BLOCK)
            blk = s[rows] if bias is None else s[rows] + bias
            s_ref[rows, :nk] = blk
            m = jnp.broadcast_to(jnp.max(blk, axis=-1, keepdims=True), (Q_BLOCK, LANE))
            m_ref[rows, :] = jnp.maximum(m, sink_ref[rows, :])

    def stage_b(tile_is_ctx, slot):
        s_ref, m_ref = slots[slot]
        m = m_ref[...]
        if tile_is_ctx:
            nk = nx
            v_all = vx_ref[...]
        else:
            nk = s_ref.shape[1]
            v_all = jnp.concatenate([vx_ref[...], vp_ref[...], vc_ref[...], vn_ref[...]], axis=0)
        v_ext = jnp.concatenate([v_all, jnp.ones_like(v_all)], axis=1)
        p = jnp.concatenate([jnp.exp2(s_ref[:, c * LANE:(c + 1) * LANE] - m).astype(BF16)
                             for c in range(nk // LANE)], axis=1)
        o = jnp.dot(p, v_ext, preferred_element_type=F32)
        o = o[:, :LANE] / (o[:, LANE:] + jnp.exp2(sink_ref[...] - m))
        for j in range(SWA_GROUP):
            pair = jnp.where(lo, o[j * Q_BLOCK:(j + 1) * Q_BLOCK],
                             o[(SWA_GROUP + j) * Q_BLOCK:(SWA_GROUP + j + 1) * Q_BLOCK])
            o_ref[:, j * LANE:(j + 1) * LANE] = pair.astype(o_ref.dtype)

    _pipeline_cases(i, lat_tiles, ctx_tiles, stage_a, stage_b)


def _swa_attention(sink_rows, qs, ks, vs, nbatch, seq_lat, seq_ctx, ctx_queries):
    qb = Q_BLOCK
    n_lat = nbatch * seq_lat
    lat_tiles = seq_lat // qb
    ctx_tiles = seq_ctx // qb if ctx_queries else 0
    n_tiles = lat_tiles + ctx_tiles
    out_rows = n_lat + (nbatch * seq_ctx if ctx_queries else 0)
    ctx_blk0 = n_lat // seq_ctx
    n_keys = seq_ctx + 3 * qb

    def tile_row(b, t):
        return jnp.where(t < lat_tiles, b * lat_tiles + t, n_lat // qb + b * ctx_tiles + (t - lat_tiles))

    def window(offset):
        def imap(b, i):
            return (b * lat_tiles + jnp.clip(i + offset, 0, lat_tiles - 1), 0)
        return pl.BlockSpec((qb, LANE), imap)

    xblk = pl.BlockSpec((seq_ctx, LANE), lambda b, i: (ctx_blk0 + b, 0))
    kern = functools.partial(_swa_kernel, lat_tiles=lat_tiles, ctx_tiles=ctx_tiles, seq_lat=seq_lat)
    width = SWA_HEADS * SWA_HD
    return pl.pallas_call(
        kern,
        grid=(nbatch, n_tiles + 1),
        in_specs=[_resident(sink_rows.shape),
                  pl.BlockSpec((qb, width), lambda b, i: (tile_row(b, jnp.minimum(i, n_tiles - 1)), 0)),
                  window(-1), window(0), window(1), xblk,
                  window(-2), window(-1), window(0), xblk],
        out_specs=pl.BlockSpec((qb, width), lambda b, i: (tile_row(b, jnp.maximum(i - 1, 0)), 0)),
        out_shape=jax.ShapeDtypeStruct((out_rows, width), BF16),
        scratch_shapes=[pltpu.VMEM((SWA_HEADS * qb, n_keys), F32), pltpu.VMEM((SWA_HEADS * qb, LANE), F32),
                        pltpu.VMEM((SWA_HEADS * qb, n_keys), F32), pltpu.VMEM((SWA_HEADS * qb, LANE), F32)],
        compiler_params=_cparams("parallel", "arbitrary"),
        name="swa_attention",
    )(sink_rows, qs, ks, ks, ks, ks, vs, vs, vs, vs)


def _pair_heads(w, axis):
    shape = w.shape
    w = w.reshape(shape[:axis] + (SWA_KV_HEADS, SWA_GROUP, SWA_HD) + shape[axis + 1:])
    w = jnp.swapaxes(w, axis, axis + 1)
    return w.reshape(shape)


def _odd_in_layout(w):
    o2 = Q_LORA + KV_LORA
    o3 = o2 + MLA_ROPE
    o4 = o3 + SWA_HEADS * SWA_HD
    kr = jnp.pad(w[..., o2:o3], ((0, 0), (0, 0), (0, LANE - MLA_ROPE)))
    return jnp.concatenate([w[..., :o2], kr, _pair_heads(w[..., o3:o4], 2), w[..., o4:]], axis=-1).astype(BF16)


def _uq_layout(w):
    per = MLA_NOPE + MLA_ROPE
    w = w.reshape(w.shape[:2] + (MLA_HEADS, per))
    w = jnp.pad(w, ((0, 0), (0, 0), (0, 0), (0, MLA_PAD - per)))
    return w.reshape(w.shape[:2] + (MLA_HEADS * MLA_PAD,)).astype(BF16)


def _out_odd_layout(w):
    half = w.shape[1] // 2
    return jnp.concatenate([w[:, :half], _pair_heads(w[:, half:], 1)], axis=1).astype(BF16)


def kernel(x, c, ctx, c_ctx, w_ada, b_ada, norm_g, w_in_even, conv_w, conv_b, conv_ln_g, conv_ln_b,
           w_in_odd, q_norm_g, kv_norm_g, w_uq, w_uk, w_uv, sink, w_out, w_gate, w_up, w_down):
    nbatch, seq, d = x.shape
    seq_ctx = ctx.shape[1]
    depth = w_ada.shape[0]
    n_lat = nbatch * seq
    n_all = n_lat + nbatch * seq_ctx

    def mod_row_for(tm):
        return lambda i: jnp.where(i < n_lat // tm, i // (seq // tm), nbatch)

    mod_row = mod_row_for(TM)

    cvec = jnp.concatenate([c, c_ctx[None, :], jnp.zeros((MOD_ROWS - nbatch - 1, d), F32)], axis=0)
    mods = _ada_mods(cvec, w_ada, b_ada).reshape(depth * MOD_ROWS * 6, 1, d)

    h = (x.reshape(n_lat, d), ctx.reshape(nbatch * seq_ctx, d))

    w_even = w_in_even.astype(BF16)
    w_odd = _odd_in_layout(w_in_odd)
    wuq, wuk, wuv = _uq_layout(w_uq), w_uk.astype(BF16), w_uv.astype(BF16)
    wo_even = w_out[0::2].astype(BF16)
    wo_odd = _out_odd_layout(w_out[1::2])
    wgu, wd = _gate_up_layout(w_gate, w_up, TH_FFN), w_down.astype(BF16)
    sink_rows = jnp.repeat(sink * LOG2E, Q_BLOCK, axis=1)[:, :, None]
    sink_rows = jnp.broadcast_to(sink_rows, sink_rows.shape[:2] + (LANE,))

    ortho = 1.0 / float((seq * FOURIER_GC) ** 0.5)
    ortho_ctx = 1.0 / float((seq_ctx * FOURIER_GC) ** 0.5)
    ct_lat, nst_lat = _dft_tables_big(seq, ortho)
    ct_ctx, nst_ctx = _dft_tables(seq_ctx, ortho_ctx)
    cc, ncs = _dft_tables(FOURIER_GC, 1.0)
    cs_c = jnp.concatenate([cc, -ncs], axis=1)
    cos_t, sin_t = _rope_tables(seq, TM)

    for l in range(depth):
        last = l == depth - 1
        j = l // 2
        g = norm_g[l][:, None, :]
        rows_out = n_lat if last else n_all
        if l % 2 == 0:
            p, zc, zs = _nm_matmul(h, g[0], mods, l, w_even, j, n_all, n_lat, mod_row, cs_c)
            y1 = _conv_branch(p, conv_w[j], conv_b[j][None], conv_ln_g[j][None], conv_ln_b[j][None],
                              n_lat, seq, seq_ctx, n_all)
            y2 = _dft_positions(ct_lat, nst_lat, ct_ctx, nst_ctx, zc, zs, nbatch, seq, seq_ctx)
            wo = wo_even
        else:
            p = _nm_matmul(h, g[0], mods, l, w_odd, j, n_all, n_lat, mod_row)
            q, k, v, qs, ks, vs = _odd_proj(p, cos_t, sin_t, q_norm_g[j][None], kv_norm_g[j][None],
                                            wuq, wuk, wuv, j, n_lat, seq, n_all)
            y1 = _mla_attention(q, k, v, nbatch, seq, seq_ctx, not last)
            y2 = _swa_attention(sink_rows[j], qs, ks, vs, nbatch, seq, seq_ctx, not last)
            wo = wo_odd
        h = _out_proj(y1, y2, wo, j, h, g[1], mods, l, rows_out, n_lat, mod_row)
        h = _ffn(h, g[2], g[3], mods, l, wgu, wd, rows_out, mod_row_for(TM_FFN))
    return h.reshape(nbatch, seq, d)
```

```python
import functools
import math

import jax
import jax.numpy as jnp
from jax import lax
from jax.experimental import pallas as pl
from jax.experimental.pallas import tpu as pltpu

F32 = jnp.float32
BF16 = jnp.bfloat16

D_MODEL = 2048
GRID_W = 64
CONV_CH = 1024
CONV_K = 31
FOURIER_CH = 1024
FOURIER_GROUPS = 4
FOURIER_GC = FOURIER_CH // FOURIER_GROUPS
MLA_NOPE = 128
MLA_ROPE = 64
MLA_V = 128
MLA_HEADS = 8
MLA_PAD = 256
Q_LORA = 512
KV_LORA = 512
SWA_HD = 64
SWA_HEADS = 16
SWA_KV_HEADS = 2
SWA_GROUP = SWA_HEADS // SWA_KV_HEADS
WINDOW = 128
Q_BLOCK = 128
ROPE_BASE = 10000.0
NORM_EPS = 1e-6
LN_EPS = 1e-5
LOG2E = math.log2(math.e)
MLA_QSCALE = (MLA_NOPE + MLA_ROPE) ** -0.5 * LOG2E
SWA_QSCALE = SWA_HD ** -0.5 * LOG2E
LANE = 128
SUBLANE = 8
HALO = 16
MOD_ROWS = 8
VMEM_LIMIT = 56 * 1024 * 1024

TM = 512
TM_FFN = 512
TH_FFN = 512
TQ_MLA = 256
MLA_HEADS_PER_STEP = 2
TT_CONV = 256
TK_DFT = 256
NEG_BIG = -1e30


def _cparams(*sem):
    return pltpu.CompilerParams(dimension_semantics=sem, vmem_limit_bytes=VMEM_LIMIT)


def _resident(shape):
    nd = len(shape)
    return pl.BlockSpec(shape, lambda *_: (0,) * nd, pipeline_mode=pl.Buffered(1))


def _layer_resident(shape, layer, block=0):
    return pl.BlockSpec((None,) + tuple(shape), lambda *_: (layer, block, 0), pipeline_mode=pl.Buffered(1))


def _rms_scale(x):
    return lax.rsqrt(jnp.mean(x * x, axis=-1, keepdims=True) + NORM_EPS)


def _rms(x, g):
    return x * _rms_scale(x) * g


def _norm_mod(x, g_ref, sh_ref, sc_ref):
    gain = g_ref[...] * (1.0 + sc_ref[...])
    return (x * _rms_scale(x) * gain + sh_ref[...]).astype(BF16)


def _gated_residual(h, y, g_ref, gate_ref):
    return h + y * _rms_scale(y) * (gate_ref[...] * g_ref[...])


def _h_operands(h, tm, n_lat):
    if isinstance(h, tuple):
        lat_tiles = n_lat // tm
        specs = [pl.BlockSpec((tm, D_MODEL), lambda i: (jnp.minimum(i, lat_tiles - 1), 0)),
                 pl.BlockSpec((tm, D_MODEL), lambda i: (jnp.maximum(i - lat_tiles, 0), 0))]
        return specs, list(h), lat_tiles
    specs = [pl.BlockSpec((tm, D_MODEL), lambda i: (i, 0)), _resident((tm, D_MODEL))]
    return specs, [h, h], None


def _tile_h(hl_ref, hc_ref, ctx_start):
    if ctx_start is None:
        return hl_ref[...]
    return jnp.where(pl.program_id(0) >= ctx_start, hc_ref[...], hl_ref[...])


def _mod_spec(layer, chunk, mod_row_of_tile):
    def imap(i, *_):
        return ((layer * MOD_ROWS + mod_row_of_tile(i)) * 6 + chunk, 0, 0)
    return pl.BlockSpec((None, 1, D_MODEL), imap)


def _ada_kernel(c_ref, w_ref, b_ref, o_ref):
    c = c_ref[...]
    a = (c * jax.nn.sigmoid(c)).astype(BF16)
    o_ref[...] = jnp.dot(a, w_ref[...].astype(BF16), preferred_element_type=F32) + b_ref[...]


def _ada_mods(cvec, w_ada, b_ada):
    depth, d, n6 = w_ada.shape
    tn = 1024
    return pl.pallas_call(
        _ada_kernel,
        grid=(depth, n6 // tn),
        in_specs=[pl.BlockSpec((MOD_ROWS, d), lambda l, j: (0, 0)),
                  pl.BlockSpec((None, d, tn), lambda l, j: (l, 0, j)),
                  pl.BlockSpec((None, 1, tn), lambda l, j: (l, 0, j))],
        out_specs=pl.BlockSpec((None, MOD_ROWS, tn), lambda l, j: (l, 0, j)),
        out_shape=jax.ShapeDtypeStruct((depth, MOD_ROWS, n6), F32),
        compiler_params=_cparams("parallel", "parallel"),
        name="ada_mods",
    )(cvec, w_ada, b_ada.reshape(depth, 1, n6))


def _nm_matmul_kernel(hl_ref, hc_ref, g_ref, sh_ref, sc_ref, w_ref, o_ref, *, ctx_start):
    u = _norm_mod(_tile_h(hl_ref, hc_ref, ctx_start), g_ref, sh_ref, sc_ref)
    o_ref[...] = jnp.dot(u, w_ref[...], preferred_element_type=F32).astype(o_ref.dtype)


def _nm_matmul_dft_kernel(hl_ref, hc_ref, g_ref, sh_ref, sc_ref, w_ref, cs_ref, o_ref, zc_ref, zs_ref,
                          *, ctx_start):
    u = _norm_mod(_tile_h(hl_ref, hc_ref, ctx_start), g_ref, sh_ref, sc_ref)
    y = jnp.dot(u, w_ref[...], preferred_element_type=F32)
    n_conv = o_ref.shape[1]
    o_ref[...] = y[:, :n_conv].astype(o_ref.dtype)
    for grp in range(FOURIER_GROUPS):
        cols = slice(grp * FOURIER_GC, (grp + 1) * FOURIER_GC)
        f = y[:, n_conv + grp * FOURIER_GC:n_conv + (grp + 1) * FOURIER_GC].astype(BF16)
        z = jnp.dot(f, cs_ref[...], preferred_element_type=F32)
        zc_ref[:, cols] = z[:, :FOURIER_GC].astype(zc_ref.dtype)
        zs_ref[:, cols] = z[:, FOURIER_GC:].astype(zs_ref.dtype)


def _nm_matmul(h, g, mods, layer, w_stack, w_layer, n_rows, n_lat, mod_row, cs_c=None):
    nout = w_stack.shape[2]

    def rows(w):
        return pl.BlockSpec((TM, w), lambda i: (i, 0))

    def out(w):
        return jax.ShapeDtypeStruct((n_rows, w), BF16)

    h_specs, h_args, ctx_start = _h_operands(h, TM, n_lat)
    in_specs = h_specs + [_resident((1, D_MODEL)),
                          _mod_spec(layer, 0, mod_row), _mod_spec(layer, 1, mod_row),
                          _layer_resident((D_MODEL, nout), w_layer)]
    args = h_args + [g, mods, mods, w_stack]
    if cs_c is None:
        kern, out_specs, out_shape = _nm_matmul_kernel, rows(nout), out(nout)
    else:
        n_conv = nout - FOURIER_CH
        kern = _nm_matmul_dft_kernel
        in_specs.append(_resident(cs_c.shape))
        args.append(cs_c)
        out_specs = [rows(n_conv), rows(FOURIER_CH), rows(FOURIER_CH)]
        out_shape = [out(n_conv), out(FOURIER_CH), out(FOURIER_CH)]
    return pl.pallas_call(
        functools.partial(kern, ctx_start=ctx_start),
        grid=(n_rows // TM,),
        in_specs=in_specs,
        out_specs=out_specs,
        out_shape=out_shape,
        compiler_params=_cparams("parallel"),
        name="nm_matmul",
    )(*args)


def _out_proj_kernel(y1_ref, y2_ref, w_ref, hl_ref, hc_ref, g_ref, gate_ref, o_ref, *, ctx_start):
    y = jnp.dot(jnp.concatenate([y1_ref[...], y2_ref[...]], axis=1), w_ref[...], preferred_element_type=F32)
    o_ref[...] = _gated_residual(_tile_h(hl_ref, hc_ref, ctx_start), y, g_ref, gate_ref)


def _out_proj(y1, y2, w_stack, w_layer, h, g, mods, layer, n_rows, n_lat, mod_row):
    half = y1.shape[1]
    h_specs, h_args, ctx_start = _h_operands(h, TM, n_lat)
    return pl.pallas_call(
        functools.partial(_out_proj_kernel, ctx_start=ctx_start),
        grid=(n_rows // TM,),
        in_specs=[pl.BlockSpec((TM, half), lambda i: (i, 0)),
                  pl.BlockSpec((TM, half), lambda i: (i, 0)),
                  _layer_resident((2 * half, D_MODEL), w_layer)] + h_specs + [
                  _resident((1, D_MODEL)),
                  _mod_spec(layer, 2, mod_row)],
        out_specs=pl.BlockSpec((TM, D_MODEL), lambda i: (i, 0)),
        out_shape=jax.ShapeDtypeStruct((n_rows, D_MODEL), F32),
        compiler_params=_cparams("parallel"),
        name="out_proj",
    )(y1, y2, w_stack, *h_args, g, mods)


def _ffn_kernel(h_ref, g2_ref, sh_ref, sc_ref, wg_ref, wu_ref, wd_ref, g3_ref, gate_ref, o_ref, u_ref):
    j = pl.program_id(1)

    @pl.when(j == 0)
    def _():
        u_ref[...] = _norm_mod(h_ref[...], g2_ref, sh_ref, sc_ref)
        o_ref[...] = jnp.zeros_like(o_ref)

    u = u_ref[...]
    a = jnp.dot(u, wg_ref[...], preferred_element_type=F32)
    b = jnp.dot(u, wu_ref[...], preferred_element_type=F32)
    hid = (a * jax.nn.sigmoid(a) * b).astype(BF16)
    o_ref[...] += jnp.dot(hid, wd_ref[...], preferred_element_type=F32)

    @pl.when(j == pl.num_programs(1) - 1)
    def _():
        o_ref[...] = _gated_residual(h_ref[...], o_ref[...], g3_ref, gate_ref)


def _ffn(h, g2, g3, mods, layer, wg, wu, wd, n_rows, mod_row):
    hidden = wg.shape[2]
    th = TH_FFN
    tm = TM_FFN
    return pl.pallas_call(
        _ffn_kernel,
        grid=(n_rows // tm, hidden // th),
        in_specs=[pl.BlockSpec((tm, D_MODEL), lambda i, j: (i, 0)),
                  _resident((1, D_MODEL)),
                  _mod_spec(layer, 3, mod_row), _mod_spec(layer, 4, mod_row),
                  pl.BlockSpec((None, D_MODEL, th), lambda i, j: (layer, 0, j)),
                  pl.BlockSpec((None, D_MODEL, th), lambda i, j: (layer, 0, j)),
                  pl.BlockSpec((None, th, D_MODEL), lambda i, j: (layer, j, 0)),
                  _resident((1, D_MODEL)),
                  _mod_spec(layer, 5, mod_row)],
        out_specs=pl.BlockSpec((tm, D_MODEL), lambda i, j: (i, 0)),
        out_shape=jax.ShapeDtypeStruct((n_rows, D_MODEL), F32),
        scratch_shapes=[pltpu.VMEM((tm, D_MODEL), BF16)],
        compiler_params=_cparams("parallel", "arbitrary"),
        name="ffn",
    )(h, g2, mods, mods, wg, wu, wd, g3, mods)


def _conv_kernel(v_ref, gt_ref, vp_ref, gp_ref, vn_ref, gn_ref, w_ref, cb_ref, lg_ref, lb_ref, o_ref,
                 buf_ref, sh_ref, acc_ref, *, lat_tiles, tiles_per_seq):
    i = pl.program_id(0)
    tt = v_ref.shape[0]
    pos = i % tiles_per_seq
    is_lat = i < lat_tiles
    has_prev = jnp.logical_and(is_lat, pos != 0)
    has_next = jnp.logical_and(is_lat, pos != tiles_per_seq - 1)

    def glu(v, g):
        return v[...].astype(F32) * jax.nn.sigmoid(g[...].astype(F32))

    buf_ref[0:HALO, :] = jnp.where(has_prev, glu(vp_ref, gp_ref), 0.0)
    buf_ref[HALO:HALO + tt, :] = glu(v_ref, gt_ref)
    buf_ref[HALO + tt:, :] = jnp.where(has_next, glu(vn_ref, gn_ref), 0.0)

    ext = sh_ref.shape[1]
    off = HALO - CONV_K // 2
    for c in range(CONV_CH // LANE):
        lanes = slice(c * LANE, (c + 1) * LANE)
        col = buf_ref[:, lanes]
        for s in range(SUBLANE):
            sh_ref[s, :, lanes] = col[s:s + ext, :]
    for c in range(CONV_CH // LANE):
        lanes = slice(c * LANE, (c + 1) * LANE)
        acc = jnp.broadcast_to(cb_ref[:, lanes], (tt, LANE))
        for k in range(CONV_K):
            s = (k + off) % SUBLANE
            base = (k + off) - s
            acc = acc + w_ref[k:k + 1, lanes] * sh_ref[s, base:base + tt, lanes]
        acc_ref[:, lanes] = acc

    rc = 32
    for r in range(tt // rc):
        x = acc_ref[r * rc:(r + 1) * rc, :]
        mu = jnp.mean(x, axis=-1, keepdims=True)
        xc = x - mu
        var = jnp.mean(xc * xc, axis=-1, keepdims=True)
        y = xc * lax.rsqrt(var + LN_EPS) * lg_ref[...] + lb_ref[...]
        o_ref[r * rc:(r + 1) * rc, :] = (y * jax.nn.sigmoid(y)).astype(o_ref.dtype)


def _conv_branch(p, conv_w, conv_b, ln_g, ln_b, n_lat, seq_lat, seq_ctx, n_rows):
    tt = TT_CONV
    assert seq_ctx == tt and seq_lat % tt == 0
    hb = tt // HALO
    last_hb = n_rows // HALO - 1
    kern = functools.partial(_conv_kernel, lat_tiles=n_lat // tt, tiles_per_seq=seq_lat // tt)

    def prev_map(col):
        return lambda i: (jnp.maximum(i * hb - 1, 0), col)

    def next_map(col):
        return lambda i: (jnp.minimum((i + 1) * hb, last_hb), col)

    return pl.pallas_call(
        kern,
        grid=(n_rows // tt,),
        in_specs=[pl.BlockSpec((tt, CONV_CH), lambda i: (i, 0)),
                  pl.BlockSpec((tt, CONV_CH), lambda i: (i, 1)),
                  pl.BlockSpec((HALO, CONV_CH), prev_map(0)),
                  pl.BlockSpec((HALO, CONV_CH), prev_map(1)),
                  pl.BlockSpec((HALO, CONV_CH), next_map(0)),
                  pl.BlockSpec((HALO, CONV_CH), next_map(1)),
                  _resident((CONV_K, CONV_CH)),
                  _resident((1, CONV_CH)), _resident((1, CONV_CH)), _resident((1, CONV_CH))],
        out_specs=pl.BlockSpec((tt, CONV_CH), lambda i: (i, 0)),
        out_shape=jax.ShapeDtypeStruct((n_rows, CONV_CH), BF16),
        scratch_shapes=[pltpu.VMEM((tt + 2 * HALO, CONV_CH), F32),
                        pltpu.VMEM((SUBLANE, tt + 2 * HALO - SUBLANE, CONV_CH), F32),
                        pltpu.VMEM((tt, CONV_CH), F32)],
        compiler_params=_cparams("parallel"),
        name="conv_branch",
    )(p, p, p, p, p, p, conv_w, conv_b, ln_g, ln_b)


def _dft_pos_kernel(ct_ref, st_ref, zc_ref, zs_ref, ctx_ct_ref, ctx_st_ref, zcx_ref, zsx_ref, o_ref, *, lat_steps):
    def run(c_ref, s_ref, a_ref, b_ref):
        y = (jnp.dot(c_ref[...], a_ref[...], preferred_element_type=F32)
             + jnp.dot(s_ref[...], b_ref[...], preferred_element_type=F32))
        o_ref[...] = y.astype(o_ref.dtype)

    is_ctx = pl.program_id(1) >= lat_steps
    pl.when(is_ctx)(functools.partial(run, ctx_ct_ref, ctx_st_ref, zcx_ref, zsx_ref))
    pl.when(jnp.logical_not(is_ctx))(functools.partial(run, ct_ref, st_ref, zc_ref, zs_ref))


def _dft_positions(ct, nst, ct_ctx, nst_ctx, zc, zs, nbatch, seq_lat, seq_ctx):
    tk = TK_DFT
    assert seq_ctx == tk and seq_lat % tk == 0
    lat_steps = seq_lat // tk
    ctx_blk0 = nbatch * lat_steps

    def tab_map(b, k):
        return (jnp.minimum(k, lat_steps - 1), 0)

    def out_map(b, k):
        return (jnp.where(k < lat_steps, b * lat_steps + k, ctx_blk0 + b), 0)

    lat_tab = pl.BlockSpec((tk, seq_lat), tab_map)
    lat_z = pl.BlockSpec((seq_lat, FOURIER_CH), lambda b, k: (b, 0))
    ctx_z = pl.BlockSpec((seq_ctx, FOURIER_CH), lambda b, k: (ctx_blk0 + b, 0))
    return pl.pallas_call(
        functools.partial(_dft_pos_kernel, lat_steps=lat_steps),
        grid=(nbatch, lat_steps + 1),
        in_specs=[lat_tab, lat_tab, lat_z, lat_z,
                  _resident((seq_ctx, seq_ctx)), _resident((seq_ctx, seq_ctx)), ctx_z, ctx_z],
        out_specs=pl.BlockSpec((tk, FOURIER_CH), out_map),
        out_shape=jax.ShapeDtypeStruct((nbatch * (seq_lat + seq_ctx), FOURIER_CH), BF16),
        compiler_params=_cparams("parallel", "arbitrary"),
        name="dft_positions",
    )(ct, nst, zc, zs, ct_ctx, nst_ctx, zc, zs)


def _dft_tables(n, scale):
    k = jnp.arange(n, dtype=jnp.int32)
    m = (k[:, None] * k[None, :]) % n
    ang = m.astype(F32) * (2.0 * jnp.pi / n)
    return (jnp.cos(ang) * scale).astype(BF16), (-jnp.sin(ang) * scale).astype(BF16)


def _dft_tables_big(n, scale):
    r = int(round(n ** 0.5))
    assert r * r == n
    t = jnp.arange(n, dtype=jnp.int32)[None, :]
    kk = jnp.arange(r, dtype=jnp.int32)[:, None]
    a_hi = ((kk * r * t) % n).astype(F32) * (2.0 * jnp.pi / n)
    a_lo = ((kk * t) % n).astype(F32) * (2.0 * jnp.pi / n)
    ch, sh, cl, sl = jnp.cos(a_hi)[:, None, :], jnp.sin(a_hi)[:, None, :], jnp.cos(a_lo)[None], jnp.sin(a_lo)[None]
    c = (ch * cl - sh * sl) * scale
    s = (sh * cl + ch * sl) * scale
    return c.reshape(n, n).astype(BF16), (-s).reshape(n, n).astype(BF16)


def _rope(x, cos, sin_signed, first_half):
    swapped = jnp.where(first_half, pltpu.roll(x, LANE - 16, 1), pltpu.roll(x, 16, 1))
    return x * cos + swapped * sin_signed


def _odd_proj_kernel(p_ref, cos_ref, sin_ref, gq_ref, gkv_ref, wuq_ref, wuk_ref, wuv_ref,
                     q_ref, k_ref, v_ref, qs_ref, ks_ref, vs_ref):
    tm = p_ref.shape[0]
    cos = cos_ref[...]
    sin = sin_ref[...]
    lane = lax.broadcasted_iota(jnp.int32, (tm, LANE), 1)
    first_half = (lane % 32) < 16
    rope = functools.partial(_rope, cos=cos, sin_signed=sin, first_half=first_half)

    o_kv = Q_LORA
    o_kr = o_kv + KV_LORA
    o_qs = o_kr + LANE
    o_ks = o_qs + SWA_HEADS * SWA_HD
    o_vs = o_ks + SWA_KV_HEADS * SWA_HD

    nq = _rms(p_ref[:, 0:Q_LORA].astype(F32), gq_ref[...]).astype(BF16)
    nkv = _rms(p_ref[:, o_kv:o_kr].astype(F32), gkv_ref[...]).astype(BF16)
    q = jnp.dot(nq, wuq_ref[...], preferred_element_type=F32) * MLA_QSCALE
    kn = jnp.dot(nkv, wuk_ref[...], preferred_element_type=F32)
    v = jnp.dot(nkv, wuv_ref[...], preferred_element_type=F32)
    kr = rope(p_ref[:, o_kr:o_qs].astype(F32)).astype(k_ref.dtype)
    ones = jnp.ones((tm, LANE), v_ref.dtype)
    for h in range(MLA_HEADS):
        c0 = h * MLA_PAD
        q_ref[:, c0:c0 + LANE] = q[:, c0:c0 + LANE].astype(q_ref.dtype)
        q_ref[:, c0 + LANE:c0 + 2 * LANE] = rope(q[:, c0 + LANE:c0 + 2 * LANE]).astype(q_ref.dtype)
        k_ref[:, c0:c0 + LANE] = kn[:, h * MLA_NOPE:(h + 1) * MLA_NOPE].astype(k_ref.dtype)
        k_ref[:, c0 + LANE:c0 + 2 * LANE] = kr
        v_ref[:, c0:c0 + LANE] = v[:, h * MLA_V:(h + 1) * MLA_V].astype(v_ref.dtype)
        v_ref[:, c0 + LANE:c0 + 2 * LANE] = ones
    for j in range(SWA_HEADS * SWA_HD // LANE):
        x = p_ref[:, o_qs + j * LANE:o_qs + (j + 1) * LANE].astype(F32) * SWA_QSCALE
        qs_ref[:, j * LANE:(j + 1) * LANE] = rope(x).astype(qs_ref.dtype)
    ks_ref[...] = rope(p_ref[:, o_ks:o_vs].astype(F32)).astype(ks_ref.dtype)
    vs_ref[...] = p_ref[:, o_vs:o_vs + LANE]


def _odd_proj(p, cos_t, sin_t, gq, gkv, wuq, wuk, wuv, layer, n_lat, seq_lat, n_rows):
    tm = TM
    lat_tiles = n_lat // tm
    per_seq = seq_lat // tm

    def tab_map(i):
        return (jnp.where(i < lat_tiles, i % per_seq, per_seq), 0)

    def rows(w):
        return pl.BlockSpec((tm, w), lambda i: (i, 0))

    def out(w):
        return jax.ShapeDtypeStruct((n_rows, w), BF16)

    pad_w = MLA_HEADS * MLA_PAD
    return pl.pallas_call(
        _odd_proj_kernel,
        grid=(n_rows // tm,),
        in_specs=[rows(p.shape[1]),
                  pl.BlockSpec((tm, LANE), tab_map), pl.BlockSpec((tm, LANE), tab_map),
                  _resident((1, Q_LORA)), _resident((1, KV_LORA)),
                  _layer_resident(wuq.shape[1:], layer), _layer_resident(wuk.shape[1:], layer),
                  _layer_resident(wuv.shape[1:], layer)],
        out_specs=[rows(pad_w), rows(pad_w), rows(pad_w), rows(SWA_HEADS * SWA_HD), rows(LANE), rows(LANE)],
        out_shape=[out(pad_w), out(pad_w), out(pad_w), out(SWA_HEADS * SWA_HD), out(LANE), out(LANE)],
        compiler_params=_cparams("parallel"),
        name="odd_proj",
    )(p, cos_t, sin_t, gq, gkv, wuq, wuk, wuv)


def _rope_tables(seq, pad_rows):
    quarter = MLA_ROPE // 4
    inv = ROPE_BASE ** (-jnp.arange(quarter, dtype=F32) / quarter)
    t = jnp.arange(seq, dtype=jnp.int32)
    row = (t // GRID_W).astype(F32)[:, None] * inv
    col = (t % GRID_W).astype(F32)[:, None] * inv
    ang = jnp.concatenate([row, row, col, col], axis=-1)
    sign = jnp.concatenate([-jnp.ones((quarter,), F32), jnp.ones((quarter,), F32)] * 2)
    cos = jnp.cos(ang)
    sin = jnp.sin(ang) * sign
    cos = jnp.concatenate([jnp.tile(cos, (1, 2)), jnp.ones((pad_rows, LANE), F32)], axis=0)
    sin = jnp.concatenate([jnp.tile(sin, (1, 2)), jnp.zeros((pad_rows, LANE), F32)], axis=0)
    return cos, sin


def _qk(q, k):
    return lax.dot_general(q, k, (((1,), (1,)), ((), ())), preferred_element_type=F32)


def _pipeline_cases(i, lat_tiles, ctx_tiles, stage_a, stage_b):
    n = lat_tiles + ctx_tiles
    groups = {}
    for step in range(n + 1):
        has_a, has_b = step < n, step >= 1
        key = (has_a, has_a and step >= lat_tiles, has_b, has_b and step - 1 >= lat_tiles, step % 2)
        groups.setdefault(key, []).append(step)
    for (has_a, a_ctx, has_b, b_ctx, slot), steps in groups.items():
        def body(has_a=has_a, a_ctx=a_ctx, has_b=has_b, b_ctx=b_ctx, slot=slot):
            if has_a:
                stage_a(a_ctx, slot)
            if has_b:
                stage_b(b_ctx, 1 - slot)
        pl.when((i >= steps[0]) & (i <= steps[-1]) & (i % 2 == slot))(body)


def _mla_kernel(q_ref, kl_ref, kc_ref, vl_ref, vc_ref, o_ref, *scratch, heads, lat_tiles, ctx_tiles):
    i = pl.program_id(2)
    n_lat_keys = kl_ref.shape[0]

    def buffers(slot, head):
        base = 2 * (slot * heads + head)
        return scratch[base], scratch[base + 1]

    def stage_a(tile_is_ctx, slot):
        for head in range(heads):
            s_ref, m_ref = buffers(slot, head)
            cols = slice(head * MLA_PAD, (head + 1) * MLA_PAD)
            q = q_ref[:, cols]
            sc = _qk(q, kc_ref[:, cols])
            m = jnp.max(sc, axis=-1, keepdims=True)
            if not tile_is_ctx:
                sl = _qk(q, kl_ref[:, cols])
                m = jnp.maximum(m, jnp.max(sl, axis=-1, keepdims=True))
                s_ref[:, :n_lat_keys] = sl
            s_ref[:, n_lat_keys:] = sc
            m_ref[...] = jnp.broadcast_to(m, m_ref.shape)

    def stage_b(tile_is_ctx, slot):
        for head in range(heads):
            s_ref, m_ref = buffers(slot, head)
            cols = slice(head * MLA_PAD, (head + 1) * MLA_PAD)
            m = m_ref[:, 0:1]
            o = jnp.dot(jnp.exp2(s_ref[:, n_lat_keys:] - m).astype(BF16), vc_ref[:, cols],
                        preferred_element_type=F32)
            if not tile_is_ctx:
                o = o + jnp.dot(jnp.exp2(s_ref[:, :n_lat_keys] - m).astype(BF16), vl_ref[:, cols],
                                preferred_element_type=F32)
            o_ref[:, head * MLA_V:(head + 1) * MLA_V] = (o[:, :MLA_V] / o[:, MLA_V:]).astype(o_ref.dtype)

    _pipeline_cases(i, lat_tiles, ctx_tiles, stage_a, stage_b)


def _mla_attention(q, k, v, nbatch, seq_lat, seq_ctx, ctx_queries):
    tq = TQ_MLA
    n_lat = nbatch * seq_lat
    lat_tiles = seq_lat // tq
    ctx_tiles = seq_ctx // tq if ctx_queries else 0
    n_tiles = lat_tiles + ctx_tiles
    out_rows = n_lat + (nbatch * seq_ctx if ctx_queries else 0)
    ctx_blk0 = n_lat // seq_ctx

    def tile_row(b, t):
        return jnp.where(t < lat_tiles, b * lat_tiles + t, n_lat // tq + b * ctx_tiles + (t - lat_tiles))

    hp = MLA_HEADS_PER_STEP
    kern = functools.partial(_mla_kernel, heads=hp, lat_tiles=lat_tiles, ctx_tiles=ctx_tiles)
    n_keys = seq_lat + seq_ctx
    width = hp * MLA_PAD
    return pl.pallas_call(
        kern,
        grid=(nbatch, MLA_HEADS // hp, n_tiles + 1),
        in_specs=[pl.BlockSpec((tq, width), lambda b, h, i: (tile_row(b, jnp.minimum(i, n_tiles - 1)), h)),
                  pl.BlockSpec((seq_lat, width), lambda b, h, i: (b, h)),
                  pl.BlockSpec((seq_ctx, width), lambda b, h, i: (ctx_blk0 + b, h)),
                  pl.BlockSpec((seq_lat, width), lambda b, h, i: (b, h)),
                  pl.BlockSpec((seq_ctx, width), lambda b, h, i: (ctx_blk0 + b, h))],
        out_specs=pl.BlockSpec((tq, hp * MLA_V), lambda b, h, i: (tile_row(b, jnp.maximum(i - 1, 0)), h)),
        out_shape=jax.ShapeDtypeStruct((out_rows, MLA_HEADS * MLA_V), BF16),
        scratch_shapes=[pltpu.VMEM((tq, n_keys), F32), pltpu.VMEM((tq, LANE), F32)] * (2 * hp),
        compiler_params=_cparams("parallel", "parallel", "arbitrary"),
        name="mla_attention",
    )(q, k, k, v, v)


def _swa_kernel(sink_ref, q_ref, kp_ref, kc_ref, kn_ref, kx_ref, vp_ref, vc_ref, vn_ref, vx_ref, o_ref,
                s0_ref, m0_ref, s1_ref, m1_ref, *, lat_tiles, ctx_tiles, seq_lat):
    i = pl.program_id(1)
    nx = kx_ref.shape[0]
    lo = lax.broadcasted_iota(jnp.int32, (Q_BLOCK, LANE), 1) < LANE // 2
    slots = ((s0_ref, m0_ref), (s1_ref, m1_ref))

    def stage_a(tile_is_ctx, slot):
        s_ref, m_ref = slots[slot]
        zero = jnp.zeros((Q_BLOCK, LANE), q_ref.dtype)
        chunks = [q_ref[:, j * LANE:(j + 1) * LANE] for j in range(SWA_GROUP)]
        qstack = jnp.concatenate([jnp.where(lo, ch, zero) for ch in chunks]
                                 + [jnp.where(lo, zero, ch) for ch in chunks], axis=0)
        if tile_is_ctx:
            s = _qk(qstack, kx_ref[...])
            bias = None
        else:
            k_all = jnp.concatenate([kx_ref[...], kp_ref[...], kc_ref[...], kn_ref[...]], axis=0)
            nk = k_all.shape[0]
            qpos = i * Q_BLOCK + lax.broadcasted_iota(jnp.int32, (Q_BLOCK, nk), 0)
            col = lax.broadcasted_iota(jnp.int32, (Q_BLOCK, nk), 1)
            kpos = (i - 1) * Q_BLOCK + (col - nx)
            in_window = (kpos >= 0) & (kpos < seq_lat) & (jnp.abs(qpos - kpos) <= WINDOW)
            bias = jnp.where((col < nx) | in_window, 0.0, NEG_BIG)
            s = _qk(qstack, k_all)
        nk = s.shape[1]
        for r in range(SWA_HEADS):
            rows = slice(r * Q_BLOCK, (r + 1) * Q_BLOCK)
            blk = s[rows] if bias is None else s[rows] + bias
            s_ref[rows, :nk] = blk
            m = jnp.broadcast_to(jnp.max(blk, axis=-1, keepdims=True), (Q_BLOCK, LANE))
            m_ref[rows, :] = jnp.maximum(m, sink_ref[rows, :])

    def stage_b(tile_is_ctx, slot):
        s_ref, m_ref = slots[slot]
        m = m_ref[...]
        if tile_is_ctx:
            nk = nx
            v_all = vx_ref[...]
        else:
            nk = s_ref.shape[1]
            v_all = jnp.concatenate([vx_ref[...], vp_ref[...], vc_ref[...], vn_ref[...]], axis=0)
        v_ext = jnp.concatenate([v_all, jnp.ones_like(v_all)], axis=1)
        p = jnp.concatenate([jnp.exp2(s_ref[:, c * LANE:(c + 1) * LANE] - m).astype(BF16)
                             for c in range(nk // LANE)], axis=1)
        o = jnp.dot(p, v_ext, preferred_element_type=F32)
        o = o[:, :LANE] / (o[:, LANE:] + jnp.exp2(sink_ref[...] - m))
        for j in range(SWA_GROUP):
            pair = jnp.where(lo, o[j * Q_BLOCK:(j + 1) * Q_BLOCK],
                             o[(SWA_GROUP + j) * Q_BLOCK:(SWA_GROUP + j + 1) * Q_BLOCK])
            o_ref[:, j * LANE:(j + 1) * LANE] = pair.astype(o_ref.dtype)

    _pipeline_cases(i, lat_tiles, ctx_tiles, stage_a, stage_b)


def _swa_attention(sink_rows, qs, ks, vs, nbatch, seq_lat, seq_ctx, ctx_queries):
    qb = Q_BLOCK
    n_lat = nbatch * seq_lat
    lat_tiles = seq_lat // qb
    ctx_tiles = seq_ctx // qb if ctx_queries else 0
    n_tiles = lat_tiles + ctx_tiles
    out_rows = n_lat + (nbatch * seq_ctx if ctx_queries else 0)
    ctx_blk0 = n_lat // seq_ctx
    n_keys = seq_ctx + 3 * qb

    def tile_row(b, t):
        return jnp.where(t < lat_tiles, b * lat_tiles + t, n_lat // qb + b * ctx_tiles + (t - lat_tiles))

    def window(offset):
        def imap(b, i):
            return (b * lat_tiles + jnp.clip(i + offset, 0, lat_tiles - 1), 0)
        return pl.BlockSpec((qb, LANE), imap)

    xblk = pl.BlockSpec((seq_ctx, LANE), lambda b, i: (ctx_blk0 + b, 0))
    kern = functools.partial(_swa_kernel, lat_tiles=lat_tiles, ctx_tiles=ctx_tiles, seq_lat=seq_lat)
    width = SWA_HEADS * SWA_HD
    return pl.pallas_call(
        kern,
        grid=(nbatch, n_tiles + 1),
        in_specs=[_resident(sink_rows.shape),
                  pl.BlockSpec((qb, width), lambda b, i: (tile_row(b, jnp.minimum(i, n_tiles - 1)), 0)),
                  window(-1), window(0), window(1), xblk,
                  window(-2), window(-1), window(0), xblk],
        out_specs=pl.BlockSpec((qb, width), lambda b, i: (tile_row(b, jnp.maximum(i - 1, 0)), 0)),
        out_shape=jax.ShapeDtypeStruct((out_rows, width), BF16),
        scratch_shapes=[pltpu.VMEM((SWA_HEADS * qb, n_keys), F32), pltpu.VMEM((SWA_HEADS * qb, LANE), F32),
                        pltpu.VMEM((SWA_HEADS * qb, n_keys), F32), pltpu.VMEM((SWA_HEADS * qb, LANE), F32)],
        compiler_params=_cparams("parallel", "arbitrary"),
        name="swa_attention",
    )(sink_rows, qs, ks, ks, ks, ks, vs, vs, vs, vs)


def _pair_heads(w, axis):
    shape = w.shape
    w = w.reshape(shape[:axis] + (SWA_KV_HEADS, SWA_GROUP, SWA_HD) + shape[axis + 1:])
    w = jnp.swapaxes(w, axis, axis + 1)
    return w.reshape(shape)


def _odd_in_layout(w):
    o2 = Q_LORA + KV_LORA
    o3 = o2 + MLA_ROPE
    o4 = o3 + SWA_HEADS * SWA_HD
    kr = jnp.pad(w[..., o2:o3], ((0, 0), (0, 0), (0, LANE - MLA_ROPE)))
    return jnp.concatenate([w[..., :o2], kr, _pair_heads(w[..., o3:o4], 2), w[..., o4:]], axis=-1).astype(BF16)


def _uq_layout(w):
    per = MLA_NOPE + MLA_ROPE
    w = w.reshape(w.shape[:2] + (MLA_HEADS, per))
    w = jnp.pad(w, ((0, 0), (0, 0), (0, 0), (0, MLA_PAD - per)))
    return w.reshape(w.shape[:2] + (MLA_HEADS * MLA_PAD,)).astype(BF16)


def _out_odd_layout(w):
    half = w.shape[1] // 2
    return jnp.concatenate([w[:, :half], _pair_heads(w[:, half:], 1)], axis=1).astype(BF16)


def kernel(x, c, ctx, c_ctx, w_ada, b_ada, norm_g, w_in_even, conv_w, conv_b, conv_ln_g, conv_ln_b,
           w_in_odd, q_norm_g, kv_norm_g, w_uq, w_uk, w_uv, sink, w_out, w_gate, w_up, w_down):
    nbatch, seq, d = x.shape
    seq_ctx = ctx.shape[1]
    depth = w_ada.shape[0]
    n_lat = nbatch * seq
    n_all = n_lat + nbatch * seq_ctx

    def mod_row_for(tm):
        return lambda i: jnp.where(i < n_lat // tm, i // (seq // tm), nbatch)

    mod_row = mod_row_for(TM)

    cvec = jnp.concatenate([c, c_ctx[None, :], jnp.zeros((MOD_ROWS - nbatch - 1, d), F32)], axis=0)
    mods = _ada_mods(cvec, w_ada, b_ada).reshape(depth * MOD_ROWS * 6, 1, d)

    h = (x.reshape(n_lat, d), ctx.reshape(nbatch * seq_ctx, d))

    w_even = w_in_even.astype(BF16)
    w_odd = _odd_in_layout(w_in_odd)
    wuq, wuk, wuv = _uq_layout(w_uq), w_uk.astype(BF16), w_uv.astype(BF16)
    wo_even = w_out[0::2].astype(BF16)
    wo_odd = _out_odd_layout(w_out[1::2])
    wg, wu, wd = w_gate.astype(BF16), w_up.astype(BF16), w_down.astype(BF16)
    sink_rows = jnp.repeat(sink * LOG2E, Q_BLOCK, axis=1)[:, :, None]
    sink_rows = jnp.broadcast_to(sink_rows, sink_rows.shape[:2] + (LANE,))

    ortho = 1.0 / float((seq * FOURIER_GC) ** 0.5)
    ortho_ctx = 1.0 / float((seq_ctx * FOURIER_GC) ** 0.5)
    ct_lat, nst_lat = _dft_tables_big(seq, ortho)
    ct_ctx, nst_ctx = _dft_tables(seq_ctx, ortho_ctx)
    cc, ncs = _dft_tables(FOURIER_GC, 1.0)
    cs_c = jnp.concatenate([cc, -ncs], axis=1)
    cos_t, sin_t = _rope_tables(seq, TM)

    for l in range(depth):
        last = l == depth - 1
        j = l // 2
        g = norm_g[l][:, None, :]
        rows_out = n_lat if last else n_all
        if l % 2 == 0:
            p, zc, zs = _nm_matmul(h, g[0], mods, l, w_even, j, n_all, n_lat, mod_row, cs_c)
            y1 = _conv_branch(p, conv_w[j], conv_b[j][None], conv_ln_g[j][None], conv_ln_b[j][None],
                              n_lat, seq, seq_ctx, n_all)
            y2 = _dft_positions(ct_lat, nst_lat, ct_ctx, nst_ctx, zc, zs, nbatch, seq, seq_ctx)
            wo = wo_even
        else:
            p = _nm_matmul(h, g[0], mods, l, w_odd, j, n_all, n_lat, mod_row)
            q, k, v, qs, ks, vs = _odd_proj(p, cos_t, sin_t, q_norm_g[j][None], kv_norm_g[j][None],
                                            wuq, wuk, wuv, j, n_lat, seq, n_all)
            y1 = _mla_attention(q, k, v, nbatch, seq, seq_ctx, not last)
            y2 = _swa_attention(sink_rows[j], qs, ks, vs, nbatch, seq, seq_ctx, not last)
            wo = wo_odd
        h = _out_proj(y1, y2, wo, j, h, g[1], mods, l, rows_out, n_lat, mod_row)
        h = _ffn(h, g[2], g[3], mods, l, wg, wu, wd, rows_out, mod_row_for(TM_FFN))
    return h.reshape(nbatch, seq, d)
```
